```python
import math
import jax, jax.numpy as jnp
from jax import lax
import numpy as np

D_MODEL = 4096
BATCH = 4
SEQ = 4096
DEPTH = 4

CHUNK = 64
N_EVEN = (DEPTH + 1) // 2
N_ODD = DEPTH // 2
D_LRU = D_MODEL // 2
LRU_BLOCK = 128
N_LRU_BLOCKS = D_LRU // LRU_BLOCK
CONV_WIDTH = 4
LRU_C = 8.0
D_POOL = D_MODEL // 2
POOL_WINDOWS = (2, 4, 8, 16)
N_POOL_GROUPS = len(POOL_WINDOWS)
POOL_GROUP = D_POOL // N_POOL_GROUPS
D_IN_EVEN = 2 * D_LRU + D_POOL
D_MIX_EVEN = D_LRU + D_POOL
D_FF_DENSE = 2 * D_MODEL
HEAD_DIM = 128
N_HEADS = D_MODEL // (2 * HEAD_DIM)
Q_BLOCK = 128
N_EXPERTS = 8
TOP_K = 2
D_FF_EXPERT = D_MODEL // 2
EPS = 1e-6
NEG_INF = -1e30

kernel_name = 'hybrid_lru_pool_diffattn_moe'


def rms_norm(x, g):
    xf = x.astype(jnp.float32)
    y = xf * lax.rsqrt(jnp.mean(xf * xf, axis=-1, keepdims=True) + EPS)
    return (y * g.astype(jnp.float32)).astype(x.dtype)


def causal_depthwise_conv(x, w, b):
    K = w.shape[0]
    S = x.shape[1]
    xp = jnp.pad(x, ((0, 0), (K - 1, 0), (0, 0)))
    y = sum(w[k] * xp[:, k:k + S] for k in range(K))
    return y + b


def rg_lru(x, w_a, b_a, w_x, b_x, lru_lambda):
    Bn, S, C = x.shape
    xb = x.reshape(Bn, S, N_LRU_BLOCKS, LRU_BLOCK)
    gate_a = jnp.einsum('bsnc,ncd->bsnd', xb, w_a).reshape(Bn, S, C) + b_a
    gate_x = jnp.einsum('bsnc,ncd->bsnd', xb, w_x).reshape(Bn, S, C) + b_x
    r = jax.nn.sigmoid(gate_a.astype(jnp.float32))
    i = jax.nn.sigmoid(gate_x.astype(jnp.float32))
    log_a = -LRU_C * r * jax.nn.softplus(-lru_lambda.astype(jnp.float32))
    a = jnp.exp(log_a)
    mult = jnp.sqrt(-jnp.expm1(2.0 * log_a))
    u = mult * i * x.astype(jnp.float32)

    def combine(left, right):
        a_l, h_l = left
        a_r, h_r = right
        return a_l * a_r, a_r * h_l + h_r

    _, h = lax.associative_scan(combine, (a, u), axis=1)
    return h.astype(x.dtype)


def multiscale_pool(x, w_pool, pool_scale):
    Bn, S, C = x.shape
    xf = x.astype(jnp.float32)
    cs = jnp.cumsum(xf, axis=1)
    outs = []
    for g, w in enumerate(POOL_WINDOWS):
        sl = slice(g * POOL_GROUP, (g + 1) * POOL_GROUP)
        c = cs[..., sl]
        lagged = jnp.pad(c, ((0, 0), (w, 0), (0, 0)))[:, :S]
        count = jnp.minimum(jnp.arange(1, S + 1), w).astype(jnp.float32)[None, :, None]
        outs.append((c - lagged) / count - xf[..., sl])
    y = jnp.stack(outs, axis=2).astype(x.dtype)
    y = jnp.einsum('bsgc,gcd->bsgd', y, w_pool).reshape(Bn, S, C)
    return y * pool_scale


def lru_pool_mixer(h, w_in, conv_w, conv_b, w_a, b_a, w_x, b_x, lru_lambda,
                   w_pool, pool_scale, w_out):
    u = h @ w_in
    x_lru = u[..., :D_LRU]
    g_lru = u[..., D_LRU:2 * D_LRU]
    x_pool = u[..., 2 * D_LRU:]
    y_lru = rg_lru(causal_depthwise_conv(x_lru, conv_w, conv_b), w_a, b_a, w_x, b_x,
                   lru_lambda) * jax.nn.gelu(g_lru)
    y_pool = multiscale_pool(x_pool, w_pool, pool_scale)
    return jnp.concatenate([y_lru, y_pool], axis=-1) @ w_out


def diff_attention(h, w_qkv, lam_vecs, subln_g, w_o, lambda_init):
    Bn, S, D = h.shape
    qkv = h @ w_qkv
    q = qkv[..., :D].reshape(Bn, S, N_HEADS, 2, HEAD_DIM)
    k = qkv[..., D:2 * D].reshape(Bn, S, N_HEADS, 2, HEAD_DIM)
    v = qkv[..., 2 * D:].reshape(Bn, S, N_HEADS, 2 * HEAD_DIM)
    lv = lam_vecs.astype(jnp.float32)
    lam = jnp.exp(jnp.sum(lv[0] * lv[1])) - jnp.exp(jnp.sum(lv[2] * lv[3])) + lambda_init
    scale = HEAD_DIM ** -0.5
    k_chunk = jnp.arange(S) // CHUNK

    def one_block(blk):
        start = blk * Q_BLOCK
        qb = lax.dynamic_slice_in_dim(q, start, Q_BLOCK, axis=1)
        s = jnp.einsum('bqhcd,bkhcd->bhcqk', qb, k).astype(jnp.float32) * scale
        q_chunk = (start + jnp.arange(Q_BLOCK)) // CHUNK
        mask = k_chunk[None, :] <= q_chunk[:, None]
        p = jax.nn.softmax(jnp.where(mask, s, NEG_INF), axis=-1)
        attn = p[:, :, 0] - lam * p[:, :, 1]
        return jnp.einsum('bhqk,bkhe->bqhe', attn.astype(v.dtype), v)

    o = lax.map(one_block, jnp.arange(S // Q_BLOCK))
    o = jnp.moveaxis(o, 0, 1).reshape(Bn, S, N_HEADS, 2 * HEAD_DIM)
    o = rms_norm(o, subln_g) * (1.0 - lambda_init)
    return o.reshape(Bn, S, D) @ w_o


def swiglu(h, w_gate, w_up, w_down):
    return (jax.nn.silu(h @ w_gate) * (h @ w_up)) @ w_down


def moe_swiglu(h, w_router, w_gate, w_up, w_down):
    Bn, S, D = h.shape
    t = h.reshape(Bn * S, D)
    logits = (t @ w_router).astype(jnp.float32)
    top_vals, top_idx = lax.top_k(logits, TOP_K)
    top_w = jax.nn.softmax(top_vals, axis=-1)
    gates = jnp.sum(jax.nn.one_hot(top_idx, N_EXPERTS, dtype=jnp.float32) * top_w[..., None],
                    axis=1)
    y = jnp.zeros_like(t)
    for e in range(N_EXPERTS):
        ye = swiglu(t, w_gate[e], w_up[e], w_down[e])
        y = y + gates[:, e:e + 1].astype(t.dtype) * ye
    return y.reshape(Bn, S, D)


def setup_inputs(seed: int = 0) -> dict:
    key = jax.random.key(seed)
    ks = iter(jax.random.split(key, 48))

    def nrm(shape, scale):
        return jax.random.normal(next(ks), shape, jnp.float32) * scale

    def gain(shape, s=0.05):
        return 1.0 + s * jax.random.normal(next(ks), shape, jnp.float32)

    E, O = N_EVEN, N_ODD
    u = jax.random.uniform(next(ks), (E, D_LRU), jnp.float32, 0.9, 0.999)
    a_base = u ** (1.0 / LRU_C)
    lru_lambda = jnp.log(a_base) - jnp.log1p(-a_base)
    return {
        'x': nrm((BATCH, SEQ, D_MODEL), 1.0),
        'ev_norm1': gain((E, D_MODEL)),
        'ev_w_in': nrm((E, D_MODEL, D_IN_EVEN), D_MODEL ** -0.5),
        'ev_conv_w': nrm((E, CONV_WIDTH, D_LRU), CONV_WIDTH ** -0.5),
        'ev_conv_b': nrm((E, D_LRU), 0.01),
        'ev_gate_a_w': nrm((E, N_LRU_BLOCKS, LRU_BLOCK, LRU_BLOCK), LRU_BLOCK ** -0.5),
        'ev_gate_a_b': nrm((E, D_LRU), 0.01),
        'ev_gate_x_w': nrm((E, N_LRU_BLOCKS, LRU_BLOCK, LRU_BLOCK), LRU_BLOCK ** -0.5),
        'ev_gate_x_b': nrm((E, D_LRU), 0.01),
        'ev_lru_lambda': lru_lambda,
        'ev_pool_w': nrm((E, N_POOL_GROUPS, POOL_GROUP, POOL_GROUP), POOL_GROUP ** -0.5),
        'ev_pool_scale': gain((E, D_POOL), 0.1),
        'ev_w_out': nrm((E, D_MIX_EVEN, D_MODEL), D_MIX_EVEN ** -0.5),
        'ev_norm2': gain((E, D_MODEL)),
        'ev_ffn_gate': nrm((E, D_MODEL, D_FF_DENSE), D_MODEL ** -0.5),
        'ev_ffn_up': nrm((E, D_MODEL, D_FF_DENSE), D_MODEL ** -0.5),
        'ev_ffn_down': nrm((E, D_FF_DENSE, D_MODEL), D_FF_DENSE ** -0.5),
        'od_norm1': gain((O, D_MODEL)),
        'od_w_qkv': nrm((O, D_MODEL, 3 * D_MODEL), D_MODEL ** -0.5),
        'od_lambda': nrm((O, 4, HEAD_DIM), 0.1),
        'od_subln': gain((O, 2 * HEAD_DIM)),
        'od_w_o': nrm((O, D_MODEL, D_MODEL), D_MODEL ** -0.5),
        'od_norm2': gain((O, D_MODEL)),
        'od_router': nrm((O, D_MODEL, N_EXPERTS), D_MODEL ** -0.5),
        'od_exp_gate': nrm((O, N_EXPERTS, D_MODEL, D_FF_EXPERT), D_MODEL ** -0.5),
        'od_exp_up': nrm((O, N_EXPERTS, D_MODEL, D_FF_EXPERT), D_MODEL ** -0.5),
        'od_exp_down': nrm((O, N_EXPERTS, D_FF_EXPERT, D_MODEL), D_FF_EXPERT ** -0.5),
        'final_norm': gain((D_MODEL,)),
    }


def reference(x, ev_norm1, ev_w_in, ev_conv_w, ev_conv_b, ev_gate_a_w, ev_gate_a_b,
              ev_gate_x_w, ev_gate_x_b, ev_lru_lambda, ev_pool_w, ev_pool_scale, ev_w_out,
              ev_norm2, ev_ffn_gate, ev_ffn_up, ev_ffn_down,
              od_norm1, od_w_qkv, od_lambda, od_subln, od_w_o, od_norm2, od_router,
              od_exp_gate, od_exp_up, od_exp_down, final_norm):
    for layer in range(DEPTH):
        j = layer // 2
        if layer % 2 == 0:
            h = rms_norm(x, ev_norm1[j])
            x = x + lru_pool_mixer(h, ev_w_in[j], ev_conv_w[j], ev_conv_b[j],
                                   ev_gate_a_w[j], ev_gate_a_b[j], ev_gate_x_w[j],
                                   ev_gate_x_b[j], ev_lru_lambda[j], ev_pool_w[j],
                                   ev_pool_scale[j], ev_w_out[j])
            h = rms_norm(x, ev_norm2[j])
            x = x + swiglu(h, ev_ffn_gate[j], ev_ffn_up[j], ev_ffn_down[j])
        else:
            lambda_init = 0.8 - 0.6 * math.exp(-0.3 * layer)
            h = rms_norm(x, od_norm1[j])
            x = x + diff_attention(h, od_w_qkv[j], od_lambda[j], od_subln[j], od_w_o[j],
                                   lambda_init)
            h = rms_norm(x, od_norm2[j])
            x = x + moe_swiglu(h, od_router[j], od_exp_gate[j], od_exp_up[j], od_exp_down[j])
    return rms_norm(x, final_norm)
```

```python
import functools
import math

import jax
import jax.numpy as jnp
from jax import lax
from jax.experimental import pallas as pl
from jax.experimental.pallas import tpu as pltpu

F32 = jnp.float32
BF16 = jnp.bfloat16

EPS = 1e-6
NEG_INF = -1e30
CHUNK = 64
LRU_C = 8.0
LRU_BLOCK = 128
POOL_WINDOWS = (2, 4, 8, 16)
HEAD_DIM = 128
TOP_K = 2

LANES = 128
SUBLANES = 8
VMEM_PHYSICAL_BYTES = 64 * 1024 * 1024
VMEM_LIMIT_BYTES = VMEM_PHYSICAL_BYTES - 8 * 1024 * 1024

CONV_HIST = SUBLANES
POOL_HIST = 2 * SUBLANES

ROW_TILE = 256
MM_TM = 1024
MM_TN = 1024
MM_TN_F32 = 512
MM_TK = 2048
SEQ_TILE = 512
LRU_CB = 512
ATTN_TILE = 512
MOE_TM = 512
DMA_ROWS = 256


def _tile(dim, pref):
    t = min(dim, pref)
    assert dim % t == 0, (dim, pref)
    return t


def _params(*semantics):
    return pltpu.CompilerParams(dimension_semantics=semantics,
                                vmem_limit_bytes=VMEM_LIMIT_BYTES)


def _rms(x, g):
    return x * lax.rsqrt(jnp.mean(x * x, axis=-1, keepdims=True) + EPS) * g


def _rmsnorm_body(x_ref, g_ref, o_ref):
    o_ref[...] = _rms(x_ref[...], g_ref[...]).astype(o_ref.dtype)


def _rmsnorm(x, g, out_dtype):
    T, D = x.shape
    tr = _tile(T, ROW_TILE)
    return pl.pallas_call(
        _rmsnorm_body,
        grid=(T // tr,),
        in_specs=[pl.BlockSpec((tr, D), lambda i: (i, 0)),
                  pl.BlockSpec((1, D), lambda i: (0, 0))],
        out_specs=pl.BlockSpec((tr, D), lambda i: (i, 0)),
        out_shape=jax.ShapeDtypeStruct((T, D), out_dtype),
        compiler_params=_params("parallel"),
        name="rmsnorm",
    )(x, g.reshape(1, D))


def _mm_body(*refs, n_pairs, has_res, nk):
    pairs = [(refs[2 * p], refs[2 * p + 1]) for p in range(n_pairs)]
    pos = 2 * n_pairs
    res_ref = refs[pos] if has_res else None
    o_ref = refs[pos + has_res]
    part = None
    for a_ref, w_ref in pairs:
        d = jnp.dot(a_ref[...], w_ref[...], preferred_element_type=F32)
        part = d if part is None else part + d

    def finish(total):
        if has_res:
            total = res_ref[...] + total
        o_ref[...] = total.astype(o_ref.dtype)

    if nk == 1:
        finish(part)
    else:
        acc_ref = refs[pos + has_res + 1]
        k = pl.program_id(2)

        @pl.when(k == 0)
        def _():
            acc_ref[...] = part

        @pl.when(k > 0)
        def _():
            acc_ref[...] += part

        @pl.when(k == nk - 1)
        def _():
            finish(acc_ref[...])


def _matmul(pairs, layer, n_out, *, res=None, out_dtype, split_k=False):
    M = pairs[0][0].shape[0]
    tm = _tile(M, MM_TM)
    tn = _tile(n_out, MM_TN if res is None else MM_TN_F32)
    if split_k:
        assert len(pairs) == 1
        tk = _tile(pairs[0][0].shape[1], MM_TK)
        tn = _tile(n_out, MM_TN)
        nk = pairs[0][0].shape[1] // tk
    else:
        nk = 1
    in_specs, args = [], []
    for a, w, rb in pairs:
        ka = a.shape[1]
        if nk == 1:
            in_specs.append(pl.BlockSpec((tm, ka), lambda i, j, k: (i, 0)))
            in_specs.append(pl.BlockSpec((None, ka, tn),
                                         lambda i, j, k, rb=rb: (layer, rb, j)))
        else:
            in_specs.append(pl.BlockSpec((tm, tk), lambda i, j, k: (i, k)))
            in_specs.append(pl.BlockSpec((None, tk, tn), lambda i, j, k: (layer, k, j)))
        args += [a, w]
    if res is not None:
        in_specs.append(pl.BlockSpec((tm, tn), lambda i, j, k: (i, j)))
        args.append(res)
    scratch = [pltpu.VMEM((tm, tn), F32)] if nk > 1 else []
    return pl.pallas_call(
        functools.partial(_mm_body, n_pairs=len(pairs), has_res=res is not None, nk=nk),
        grid=(M // tm, n_out // tn, nk),
        in_specs=in_specs,
        out_specs=pl.BlockSpec((tm, tn), lambda i, j, k: (i, j)),
        out_shape=jax.ShapeDtypeStruct((M, n_out), out_dtype),
        scratch_shapes=scratch,
        compiler_params=_params("parallel", "parallel", "arbitrary"),
        name="matmul",
    )(*args)


def _swiglu_body(a_ref, wg_ref, wu_ref, o_ref):
    a = a_ref[...]
    g = jnp.dot(a, wg_ref[...], preferred_element_type=F32)
    u = jnp.dot(a, wu_ref[...], preferred_element_type=F32)
    o_ref[...] = (jax.nn.silu(g) * u).astype(o_ref.dtype)


def _swiglu_up(a, wg, wu, layer):
    M, K = a.shape
    N = wg.shape[2]
    tm = _tile(M, MM_TM)
    tn = _tile(N, MM_TN_F32)
    return pl.pallas_call(
        _swiglu_body,
        grid=(M // tm, N // tn),
        in_specs=[pl.BlockSpec((tm, K), lambda i, j: (i, 0)),
                  pl.BlockSpec((None, K, tn), lambda i, j: (layer, 0, j)),
                  pl.BlockSpec((None, K, tn), lambda i, j: (layer, 0, j))],
        out_specs=pl.BlockSpec((tm, tn), lambda i, j: (i, j)),
        out_shape=jax.ShapeDtypeStruct((M, N), BF16),
        compiler_params=_params("parallel", "parallel"),
        name="swiglu_up",
    )(a, wg, wu)


def _lru_body(x_ref, g_ref, cw_ref, cb_ref, wa_ref, ba_ref, wx_ref, bx_ref, lam_ref, o_ref,
              xpad_ref, a_scr, u_scr, h_scr, hstate_ref, *, ts, nblk):
    s = pl.program_id(2)

    @pl.when(s == 0)
    def _():
        xpad_ref[0:CONV_HIST, :] = jnp.zeros((CONV_HIST, xpad_ref.shape[1]), F32)
        hstate_ref[...] = jnp.zeros(hstate_ref.shape, F32)

    xpad_ref[CONV_HIST:CONV_HIST + ts, :] = x_ref[...]
    kw = cw_ref.shape[0]
    xc = cw_ref[kw - 1:kw, :] * x_ref[...]
    for k in range(kw - 1):
        off = CONV_HIST - (kw - 1) + k
        xc = xc + cw_ref[k:k + 1, :] * xpad_ref[off:off + ts, :]
    xc = xc + cb_ref[...]
    xpad_ref[0:CONV_HIST, :] = x_ref[ts - CONV_HIST:ts, :]

    lam = lam_ref[...]
    sp = jnp.maximum(-lam, 0.0) + jnp.log1p(jnp.exp(-jnp.abs(lam)))
    xcb = xc.astype(BF16)
    for n in range(nblk):
        sl = slice(n * LRU_BLOCK, (n + 1) * LRU_BLOCK)
        ga = jnp.dot(xcb[:, sl], wa_ref[n], preferred_element_type=F32) + ba_ref[:, sl]
        gx = jnp.dot(xcb[:, sl], wx_ref[n], preferred_element_type=F32) + bx_ref[:, sl]
        log_a = -LRU_C * jax.nn.sigmoid(ga) * sp[:, sl]
        th = jnp.tanh(log_a)
        mult = jnp.sqrt(-2.0 * th / (1.0 - th))
        a_scr[:, sl] = jnp.exp(log_a)
        u_scr[:, sl] = mult * jax.nn.sigmoid(gx) * xc[:, sl]

    def step(t, h):
        h = a_scr[pl.ds(t, 1), :] * h + u_scr[pl.ds(t, 1), :]
        h_scr[pl.ds(t, 1), :] = h
        return h

    hstate_ref[...] = lax.fori_loop(0, ts, step, hstate_ref[...], unroll=8)
    o_ref[...] = (h_scr[...] * jax.nn.gelu(g_ref[...])).astype(o_ref.dtype)


def _lru(u3, layer, d_lru, conv_w, conv_b, wa, ba, wx, bx, lam):
    B, S, _ = u3.shape
    ts = _tile(S, SEQ_TILE)
    cb = _tile(d_lru, LRU_CB)
    nblk = cb // LRU_BLOCK
    kw = conv_w.shape[1]
    gate_col0 = d_lru // cb
    vec = lambda: pl.BlockSpec((None, 1, cb), lambda b, c, s: (layer, 0, c))
    gate_w = lambda: pl.BlockSpec((None, nblk, LRU_BLOCK, LRU_BLOCK),
                                  lambda b, c, s: (layer, c, 0, 0))
    return pl.pallas_call(
        functools.partial(_lru_body, ts=ts, nblk=nblk),
        grid=(B, d_lru // cb, S // ts),
        in_specs=[pl.BlockSpec((None, ts, cb), lambda b, c, s: (b, s, c)),
                  pl.BlockSpec((None, ts, cb), lambda b, c, s: (b, s, gate_col0 + c)),
                  pl.BlockSpec((None, kw, cb), lambda b, c, s: (layer, 0, c)),
                  vec(), gate_w(), vec(), gate_w(), vec(), vec()],
        out_specs=pl.BlockSpec((None, ts, cb), lambda b, c, s: (b, s, c)),
        out_shape=jax.ShapeDtypeStruct((B, S, d_lru), BF16),
        scratch_shapes=[pltpu.VMEM((CONV_HIST + ts, cb), F32),
                        pltpu.VMEM((ts, cb), F32),
                        pltpu.VMEM((ts, cb), F32),
                        pltpu.VMEM((ts, cb), F32),
                        pltpu.VMEM((1, cb), F32)],
        compiler_params=_params("parallel", "parallel", "arbitrary"),
        name="lru",
    )(u3, u3, conv_w, conv_b, wa, ba, wx, bx, lam)


def _pool_body(x_ref, w_ref, sc_ref, o_ref, xpad_ref, *, ts, gw):
    s = pl.program_id(1)

    @pl.when(s == 0)
    def _():
        xpad_ref[0:POOL_HIST, :] = jnp.zeros((POOL_HIST, xpad_ref.shape[1]), F32)

    xpad_ref[POOL_HIST:POOL_HIST + ts, :] = x_ref[...]
    frames = (s * ts + 1 + lax.broadcasted_iota(jnp.int32, (ts, 1), 0)).astype(F32)
    for g, win in enumerate(POOL_WINDOWS):
        cs = slice(g * gw, (g + 1) * gw)
        acc = x_ref[:, cs]
        for k in range(1, win):
            acc = acc + xpad_ref[POOL_HIST - k:POOL_HIST - k + ts, cs]
        y = acc / jnp.minimum(frames, float(win)) - x_ref[:, cs]
        z = jnp.dot(y.astype(BF16), w_ref[g], preferred_element_type=F32) * sc_ref[:, cs]
        o_ref[:, cs] = z.astype(o_ref.dtype)
    xpad_ref[0:POOL_HIST, :] = x_ref[ts - POOL_HIST:ts, :]


def _pool(u3, layer, col_block, d_pool, w_pool, scale):
    B, S, _ = u3.shape
    ts = _tile(S, SEQ_TILE)
    ng = len(POOL_WINDOWS)
    gw = d_pool // ng
    assert max(POOL_WINDOWS) <= POOL_HIST <= ts
    return pl.pallas_call(
        functools.partial(_pool_body, ts=ts, gw=gw),
        grid=(B, S // ts),
        in_specs=[pl.BlockSpec((None, ts, d_pool), lambda b, s: (b, s, col_block)),
                  pl.BlockSpec((None, ng, gw, gw), lambda b, s: (layer, 0, 0, 0)),
                  pl.BlockSpec((None, 1, d_pool), lambda b, s: (layer, 0, 0))],
        out_specs=pl.BlockSpec((None, ts, d_pool), lambda b, s: (b, s, 0)),
        out_shape=jax.ShapeDtypeStruct((B, S, d_pool), BF16),
        scratch_shapes=[pltpu.VMEM((POOL_HIST + ts, d_pool), F32)],
        compiler_params=_params("parallel", "arbitrary"),
        name="pool",
    )(u3, w_pool, scale)


def _attn_tile(q_ref, k_ref, v_ref, m_scr, l_scr, acc_scr, *, masked, c_exp):
    v = v_ref[...]
    for c in range(2):
        cs = slice(c * HEAD_DIM, (c + 1) * HEAD_DIM)
        s = lax.dot_general(q_ref[:, cs], k_ref[:, cs], (((1,), (1,)), ((), ())),
                            preferred_element_type=F32)
        if masked:
            shift = CHUNK.bit_length() - 1
            qc = lax.shift_right_logical(lax.broadcasted_iota(jnp.int32, s.shape, 0), shift)
            kc = lax.shift_right_logical(lax.broadcasted_iota(jnp.int32, s.shape, 1), shift)
            s = jnp.where(kc <= qc, s, NEG_INF)
        m_prev = m_scr[c]
        m_new = jnp.maximum(m_prev, jnp.max(s, axis=1, keepdims=True))
        alpha = jnp.exp2((m_prev - m_new) * c_exp)
        p = jnp.exp2((s - m_new) * c_exp)
        l_scr[c] = alpha * l_scr[c] + jnp.sum(p, axis=1, keepdims=True)
        acc_scr[c] = alpha * acc_scr[c] + jnp.dot(p.astype(BF16), v, preferred_element_type=F32)
        m_scr[c] = m_new


def _attn_body(lam_ref, g_ref, q_ref, k_ref, v_ref, o_ref, m_scr, l_scr, acc_scr, *,
               c_exp, lambda_init):
    i = pl.program_id(2)
    j = pl.program_id(3)

    @pl.when(j == 0)
    def _():
        m_scr[...] = jnp.full(m_scr.shape, NEG_INF, F32)
        l_scr[...] = jnp.zeros(l_scr.shape, F32)
        acc_scr[...] = jnp.zeros(acc_scr.shape, F32)

    tile = functools.partial(_attn_tile, q_ref, k_ref, v_ref, m_scr, l_scr, acc_scr, c_exp=c_exp)

    @pl.when(j < i)
    def _():
        tile(masked=False)

    @pl.when(j == i)
    def _():
        tile(masked=True)
        lv = lam_ref[...]
        lam = (jnp.exp(jnp.sum(lv[0:1] * lv[1:2], axis=1, keepdims=True))
               - jnp.exp(jnp.sum(lv[2:3] * lv[3:4], axis=1, keepdims=True)) + lambda_init)
        o = acc_scr[0] / l_scr[0] - lam * (acc_scr[1] / l_scr[1])
        o_ref[...] = (_rms(o, g_ref[...]) * (1.0 - lambda_init)).astype(o_ref.dtype)


def _attention(qkv3, layer, lam_vecs, subln_g, lambda_init):
    B, S, d3 = qkv3.shape
    D = d3 // 3
    hw = 2 * HEAD_DIM
    H = D // hw
    t = _tile(S, ATTN_TILE)
    assert CHUNK & (CHUNK - 1) == 0 and t % CHUNK == 0
    c_exp = HEAD_DIM ** -0.5 * math.log2(math.e)
    return pl.pallas_call(
        functools.partial(_attn_body, c_exp=c_exp, lambda_init=lambda_init),
        grid=(B, H, S // t, S // t),
        in_specs=[pl.BlockSpec((None, 4, HEAD_DIM), lambda b, h, i, j: (layer, 0, 0)),
                  pl.BlockSpec((None, 1, hw), lambda b, h, i, j: (layer, 0, 0)),
                  pl.BlockSpec((None, t, hw), lambda b, h, i, j: (b, i, h)),
                  pl.BlockSpec((None, t, hw), lambda b, h, i, j: (b, jnp.minimum(j, i), H + h)),
                  pl.BlockSpec((None, t, hw),
                               lambda b, h, i, j: (b, jnp.minimum(j, i), 2 * H + h))],
        out_specs=pl.BlockSpec((None, t, hw), lambda b, h, i, j: (b, i, h)),
        out_shape=jax.ShapeDtypeStruct((B, S, D), BF16),
        scratch_shapes=[pltpu.VMEM((2, t, 1), F32),
                        pltpu.VMEM((2, t, 1), F32),
                        pltpu.VMEM((2, t, hw), F32)],
        compiler_params=_params("parallel", "parallel", "parallel", "arbitrary"),
        name="diff_attention",
    )(lam_vecs, subln_g, qkv3, qkv3, qkv3)


ROUTE_IDX0, ROUTE_IDX1, ROUTE_W0, ROUTE_W1 = 0, 1, 2, 3


def _router_body(x_ref, g_ref, wr_ref, h_ref, route_ref, *, n_experts):
    h = _rms(x_ref[...], g_ref[...])
    h_ref[...] = h.astype(h_ref.dtype)
    logits = jnp.dot(h, wr_ref[...], preferred_element_type=F32,
                     precision=lax.Precision.HIGHEST)
    col = lax.broadcasted_iota(jnp.int32, logits.shape, 1)
    logits = jnp.where(col < n_experts, logits, -jnp.inf)
    v0 = jnp.max(logits, axis=1, keepdims=True)
    i0 = jnp.min(jnp.where(logits == v0, col, LANES), axis=1, keepdims=True)
    rest = jnp.where(col == i0, -jnp.inf, logits)
    v1 = jnp.max(rest, axis=1, keepdims=True)
    i1 = jnp.min(jnp.where(rest == v1, col, LANES), axis=1, keepdims=True)
    e = jnp.exp(v1 - v0)
    w0 = 1.0 / (1.0 + e)
    w1 = e / (1.0 + e)
    route = jnp.where(col == ROUTE_IDX0, i0.astype(F32), 0.0)
    route = jnp.where(col == ROUTE_IDX1, i1.astype(F32), route)
    route = jnp.where(col == ROUTE_W0, w0, route)
    route = jnp.where(col == ROUTE_W1, w1, route)
    route_ref[...] = route


def _norm_router(x, g, w_router_padded, n_experts):
    T, D = x.shape
    tr = _tile(T, ROW_TILE)
    return pl.pallas_call(
        functools.partial(_router_body, n_experts=n_experts),
        grid=(T // tr,),
        in_specs=[pl.BlockSpec((tr, D), lambda i: (i, 0)),
                  pl.BlockSpec((1, D), lambda i: (0, 0)),
                  pl.BlockSpec((D, LANES), lambda i: (0, 0))],
        out_specs=[pl.BlockSpec((tr, D), lambda i: (i, 0)),
                   pl.BlockSpec((tr, LANES), lambda i: (i, 0))],
        out_shape=[jax.ShapeDtypeStruct((T, D), BF16),
                   jax.ShapeDtypeStruct((T, LANES), F32)],
        compiler_params=_params("parallel"),
        name="norm_router",
    )(x, g.reshape(1, D), w_router_padded)


def _slot_plan(expert_idx, n_experts, tm):
    T = expert_idx.shape[0]
    flat = expert_idx.reshape(T * TOP_K)
    onehot = (flat[:, None] == jnp.arange(n_experts, dtype=jnp.int32)[None, :]).astype(jnp.int32)
    csum = jnp.cumsum(onehot, axis=0)
    pos = jnp.sum(csum * onehot, axis=1) - 1
    counts = csum[-1]
    padded = (counts + tm - 1) // tm * tm
    gend = jnp.cumsum(padded)
    gstart = gend - padded
    slot = (jnp.sum(onehot * gstart[None, :], axis=1) + pos).astype(jnp.int32)
    n_tiles = (T * TOP_K) // tm + n_experts
    tile_start = jnp.arange(n_tiles, dtype=jnp.int32) * tm
    tile_expert = jnp.sum((tile_start[:, None] >= gend[None, :]).astype(jnp.int32), axis=1)
    n_valid = (gend[-1] // tm).astype(jnp.int32).reshape(1)
    tile_expert = jnp.minimum(tile_expert, n_experts - 1).astype(jnp.int32)
    return slot, tile_expert, n_valid, n_tiles


def _row_copy_body(slot_ref, src_ref, dst_in_ref, dst_ref, sem, *, rows, to_slots):
    del dst_in_ref
    base = pl.program_id(0) * rows

    def copy(r, k):
        t = base + r
        s = slot_ref[TOP_K * t + k]
        if to_slots:
            return pltpu.make_async_copy(src_ref.at[t], dst_ref.at[s], sem)
        return pltpu.make_async_copy(src_ref.at[s], dst_ref.at[k, t], sem)

    def start(r, carry):
        for k in range(TOP_K):
            copy(r, k).start()
        return carry

    def wait(r, carry):
        for k in range(TOP_K):
            copy(r, k).wait()
        return carry

    lax.fori_loop(0, rows, start, 0)
    lax.fori_loop(0, rows, wait, 0)


def _scatter_rows(slot, src3, n_slots):
    T, sub, lanes = src3.shape
    rows = _tile(T, DMA_ROWS)
    dst0 = jnp.zeros((n_slots, sub, lanes), src3.dtype)
    return pl.pallas_call(
        functools.partial(_row_copy_body, rows=rows, to_slots=True),
        grid_spec=pltpu.PrefetchScalarGridSpec(
            num_scalar_prefetch=1,
            grid=(T // rows,),
            in_specs=[pl.BlockSpec(memory_space=pl.ANY), pl.BlockSpec(memory_space=pl.ANY)],
            out_specs=pl.BlockSpec(memory_space=pl.ANY),
            scratch_shapes=[pltpu.SemaphoreType.DMA(())]),
        out_shape=jax.ShapeDtypeStruct((n_slots, sub, lanes), src3.dtype),
        input_output_aliases={2: 0},
        compiler_params=_params("arbitrary"),
        name="scatter_rows",
    )(slot, src3, dst0)


def _gather_body(slot_ref, src_ref, dst_ref, sem, *, rows):
    _row_copy_body(slot_ref, src_ref, None, dst_ref, sem, rows=rows, to_slots=False)


def _gather_rows(slot, src3, T):
    _, sub, lanes = src3.shape
    rows = _tile(T, DMA_ROWS)
    return pl.pallas_call(
        functools.partial(_gather_body, rows=rows),
        grid_spec=pltpu.PrefetchScalarGridSpec(
            num_scalar_prefetch=1,
            grid=(T // rows,),
            in_specs=[pl.BlockSpec(memory_space=pl.ANY)],
            out_specs=pl.BlockSpec(memory_space=pl.ANY),
            scratch_shapes=[pltpu.SemaphoreType.DMA(())]),
        out_shape=jax.ShapeDtypeStruct((TOP_K, T, sub, lanes), src3.dtype),
        compiler_params=_params("arbitrary"),
        name="gather_rows",
    )(slot, src3)


def _grouped_body(te_ref, nv_ref, *refs, body):
    del te_ref
    valid = pl.program_id(1) < nv_ref[0]
    o_ref = refs[-1]

    @pl.when(valid)
    def _():
        body(*refs)

    @pl.when(jnp.logical_not(valid))
    def _():
        o_ref[...] = jnp.zeros(o_ref.shape, o_ref.dtype)


def _down_body(a_ref, w_ref, o_ref):
    o_ref[...] = jnp.dot(a_ref[...], w_ref[...], preferred_element_type=F32).astype(o_ref.dtype)


def _grouped_call(body, tile_expert, n_valid, a, weights, layer, n_tiles, tm, name):
    P, K = a.shape
    N = weights[0].shape[3]
    tn = _tile(N, MM_TN)
    row = lambda n, i, te, nv: jnp.minimum(i, nv[0] - 1)
    w_spec = pl.BlockSpec((None, None, K, tn),
                          lambda n, i, te, nv: (layer, te[row(n, i, te, nv)], 0, n))
    return pl.pallas_call(
        functools.partial(_grouped_body, body=body),
        grid_spec=pltpu.PrefetchScalarGridSpec(
            num_scalar_prefetch=2,
            grid=(N // tn, n_tiles),
            in_specs=[pl.BlockSpec((tm, K), lambda n, i, te, nv: (row(n, i, te, nv), 0))]
            + [w_spec] * len(weights),
            out_specs=pl.BlockSpec((tm, tn), lambda n, i, te, nv: (i, n))),
        out_shape=jax.ShapeDtypeStruct((P, N), BF16),
        compiler_params=_params("parallel", "arbitrary"),
        name=name,
    )(tile_expert, n_valid, a, *weights)


def _combine_body(x_ref, route_ref, y0_ref, y1_ref, g_ref, *out_refs, emit_x):
    w0 = route_ref[:, ROUTE_W0:ROUTE_W0 + 1]
    w1 = route_ref[:, ROUTE_W1:ROUTE_W1 + 1]
    x = x_ref[...] + (w0 * y0_ref[...].astype(F32) + w1 * y1_ref[...].astype(F32))
    if emit_x:
        out_refs[0][...] = x
    h_ref = out_refs[-1]
    h_ref[...] = _rms(x, g_ref[...]).astype(h_ref.dtype)


def _combine_norm(x, route, y, g, *, emit_x, h_dtype):
    T, D = x.shape
    tr = _tile(T, ROW_TILE)
    row = pl.BlockSpec((tr, D), lambda i: (i, 0))
    outs = ([jax.ShapeDtypeStruct((T, D), F32)] if emit_x else []) + \
        [jax.ShapeDtypeStruct((T, D), h_dtype)]
    return pl.pallas_call(
        functools.partial(_combine_body, emit_x=emit_x),
        grid=(T // tr,),
        in_specs=[row,
                  pl.BlockSpec((tr, LANES), lambda i: (i, 0)),
                  pl.BlockSpec((None, tr, D), lambda i: (0, i, 0)),
                  pl.BlockSpec((None, tr, D), lambda i: (1, i, 0)),
                  pl.BlockSpec((1, D), lambda i: (0, 0))],
        out_specs=[row] * len(outs),
        out_shape=outs,
        compiler_params=_params("parallel"),
        name="combine_norm",
    )(x, route, y, y, g.reshape(1, D))


def _moe(x, norm_g, w_router, w_gate, w_up, w_down, layer, next_g, *, emit_x, h_dtype):
    T, D = x.shape
    n_experts = w_router.shape[2]
    tm = _tile(T * TOP_K // n_experts, MOE_TM)
    wr = jnp.pad(w_router[layer], ((0, 0), (0, LANES - n_experts)))
    h, route = _norm_router(x, norm_g, wr, n_experts)
    expert_idx = route[:, ROUTE_IDX0:ROUTE_IDX1 + 1].astype(jnp.int32)
    slot, tile_expert, n_valid, n_tiles = _slot_plan(expert_idx, n_experts, tm)
    n_slots = n_tiles * tm
    sub = D // LANES
    xg = _scatter_rows(slot, h.reshape(T, sub, LANES), n_slots).reshape(n_slots, D)
    act = _grouped_call(_swiglu_body, tile_expert, n_valid, xg, [w_gate, w_up], layer,
                        n_tiles, tm, "moe_up")
    y = _grouped_call(_down_body, tile_expert, n_valid, act, [w_down], layer,
                      n_tiles, tm, "moe_down")
    yt = _gather_rows(slot, y.reshape(n_slots, sub, LANES), T).reshape(TOP_K, T, D)
    return _combine_norm(x, route, yt, next_g, emit_x=emit_x, h_dtype=h_dtype)


def kernel(x, ev_norm1, ev_w_in, ev_conv_w, ev_conv_b, ev_gate_a_w, ev_gate_a_b, ev_gate_x_w, ev_gate_x_b, ev_lru_lambda, ev_pool_w, ev_pool_scale, ev_w_out, ev_norm2, ev_ffn_gate, ev_ffn_up, ev_ffn_down, od_norm1, od_w_qkv, od_lambda, od_subln, od_w_o, od_norm2, od_router, od_exp_gate, od_exp_up, od_exp_down, final_norm):
    B, S, D = x.shape
    T = B * S
    n_even, n_odd = ev_norm1.shape[0], od_norm1.shape[0]
    depth = n_even + n_odd
    d_lru = ev_lru_lambda.shape[1]
    d_pool = ev_pool_scale.shape[1]

    w_in, w_out = ev_w_in.astype(BF16), ev_w_out.astype(BF16)
    gate_a_w, gate_x_w = ev_gate_a_w.astype(BF16), ev_gate_x_w.astype(BF16)
    pool_w = ev_pool_w.astype(BF16)
    ffn_gate, ffn_up, ffn_down = (w.astype(BF16) for w in (ev_ffn_gate, ev_ffn_up, ev_ffn_down))
    w_qkv, w_o = od_w_qkv.astype(BF16), od_w_o.astype(BF16)
    exp_gate, exp_up, exp_down = (w.astype(BF16) for w in (od_exp_gate, od_exp_up, od_exp_down))
    row3 = lambda v: v.reshape(v.shape[0], 1, v.shape[1])

    xs = x.reshape(T, D)
    h = None
    out = None
    for layer in range(depth):
        j = layer // 2
        if layer % 2 == 0:
            if h is None:
                h = _rmsnorm(xs, ev_norm1[j], BF16)
            u3 = _matmul([(h, w_in, 0)], j, w_in.shape[2], out_dtype=F32).reshape(B, S, -1)
            y_lru = _lru(u3, j, d_lru, ev_conv_w, row3(ev_conv_b), gate_a_w, row3(ev_gate_a_b),
                         gate_x_w, row3(ev_gate_x_b), row3(ev_lru_lambda))
            assert (2 * d_lru) % d_pool == 0
            y_pool = _pool(u3, j, 2 * d_lru // d_pool, d_pool, pool_w, row3(ev_pool_scale))
            assert d_lru == d_pool
            xs = _matmul([(y_lru.reshape(T, d_lru), w_out, 0), (y_pool.reshape(T, d_pool), w_out, 1)],
                         j, D, res=xs, out_dtype=F32)
            h = _rmsnorm(xs, ev_norm2[j], BF16)
            act = _swiglu_up(h, ffn_gate, ffn_up, j)
            xs = _matmul([(act, ffn_down, 0)], j, D, res=xs, out_dtype=F32, split_k=True)
            h = None
        else:
            lambda_init = 0.8 - 0.6 * math.exp(-0.3 * layer)
            if h is None:
                h = _rmsnorm(xs, od_norm1[j], BF16)
            qkv3 = _matmul([(h, w_qkv, 0)], j, 3 * D, out_dtype=BF16).reshape(B, S, 3 * D)
            o = _attention(qkv3, j, od_lambda, row3(od_subln), lambda_init)
            xs = _matmul([(o.reshape(T, D), w_o, 0)], j, D, res=xs, out_dtype=F32)
            last = layer == depth - 1
            if last:
                next_g = final_norm
            else:
                next_g = ev_norm1[j + 1]
            res = _moe(xs, od_norm2[j], od_router, exp_gate, exp_up, exp_down, j, next_g,
                       emit_x=not last, h_dtype=F32 if last else BF16)
            if last:
                out = res[0]
            else:
                xs, h = res
    if out is None:
        out = _rmsnorm(xs, final_norm, F32)
    return out.reshape(B, S, D)
```

```python
import functools
import math

import jax
import jax.numpy as jnp
import numpy as np
from jax import lax
from jax.experimental import pallas as pl
from jax.experimental.pallas import tpu as pltpu

F32 = jnp.float32
BF16 = jnp.bfloat16

EPS = 1e-6
NEG_INF = -1e30
CHUNK = 64
LRU_C = 8.0
LRU_BLOCK = 128
POOL_WINDOWS = (2, 4, 8, 16)
HEAD_DIM = 128
TOP_K = 2

LANES = 128
SUBLANES = 8
VMEM_PHYSICAL_BYTES = 64 * 1024 * 1024
VMEM_LIMIT_BYTES = VMEM_PHYSICAL_BYTES - 8 * 1024 * 1024

CONV_HIST = SUBLANES
POOL_HIST = 2 * SUBLANES

ROW_TILE = 256
MM_TM = 1024
MM_TN = 1024
MM_TN_F32 = 512
MM_TK = 2048
SEQ_TILE = 512
LRU_CB = 512
ATTN_TILE = 512
MOE_TM = 512
DMA_ROWS = 256


def _tile(dim, pref):
    t = min(dim, pref)
    assert dim % t == 0, (dim, pref)
    return t


def _params(*semantics):
    return pltpu.CompilerParams(dimension_semantics=semantics,
                                vmem_limit_bytes=VMEM_LIMIT_BYTES)


def _rms(x, g):
    return x * lax.rsqrt(jnp.mean(x * x, axis=-1, keepdims=True) + EPS) * g


def _rmsnorm_body(x_ref, g_ref, o_ref):
    o_ref[...] = _rms(x_ref[...], g_ref[...]).astype(o_ref.dtype)


def _rmsnorm(x, g, out_dtype):
    T, D = x.shape
    tr = _tile(T, ROW_TILE)
    return pl.pallas_call(
        _rmsnorm_body,
        grid=(T // tr,),
        in_specs=[pl.BlockSpec((tr, D), lambda i: (i, 0)),
                  pl.BlockSpec((1, D), lambda i: (0, 0))],
        out_specs=pl.BlockSpec((tr, D), lambda i: (i, 0)),
        out_shape=jax.ShapeDtypeStruct((T, D), out_dtype),
        compiler_params=_params("parallel"),
        name="rmsnorm",
    )(x, g.reshape(1, D))


def _mm_body(*refs, n_pairs, has_res, nk):
    pairs = [(refs[2 * p], refs[2 * p + 1]) for p in range(n_pairs)]
    pos = 2 * n_pairs
    res_ref = refs[pos] if has_res else None
    o_ref = refs[pos + has_res]
    part = None
    for a_ref, w_ref in pairs:
        d = jnp.dot(a_ref[...], w_ref[...], preferred_element_type=F32)
        part = d if part is None else part + d

    def finish(total):
        if has_res:
            total = res_ref[...] + total
        o_ref[...] = total.astype(o_ref.dtype)

    if nk == 1:
        finish(part)
    else:
        acc_ref = refs[pos + has_res + 1]
        k = pl.program_id(2)

        @pl.when(k == 0)
        def _():
            acc_ref[...] = part

        @pl.when(k > 0)
        def _():
            acc_ref[...] += part

        @pl.when(k == nk - 1)
        def _():
            finish(acc_ref[...])


def _matmul(pairs, layer, n_out, *, res=None, out_dtype, split_k=False):
    M = pairs[0][0].shape[0]
    tm = _tile(M, MM_TM)
    tn = _tile(n_out, MM_TN if res is None else MM_TN_F32)
    if split_k:
        assert len(pairs) == 1
        tk = _tile(pairs[0][0].shape[1], MM_TK)
        tn = _tile(n_out, MM_TN)
        nk = pairs[0][0].shape[1] // tk
    else:
        nk = 1
    in_specs, args = [], []
    for a, w, rb in pairs:
        ka = a.shape[1]
        if nk == 1:
            in_specs.append(pl.BlockSpec((tm, ka), lambda i, j, k: (i, 0)))
            in_specs.append(pl.BlockSpec((None, ka, tn),
                                         lambda i, j, k, rb=rb: (layer, rb, j)))
        else:
            in_specs.append(pl.BlockSpec((tm, tk), lambda i, j, k: (i, k)))
            in_specs.append(pl.BlockSpec((None, tk, tn), lambda i, j, k: (layer, k, j)))
        args += [a, w]
    if res is not None:
        in_specs.append(pl.BlockSpec((tm, tn), lambda i, j, k: (i, j)))
        args.append(res)
    scratch = [pltpu.VMEM((tm, tn), F32)] if nk > 1 else []
    return pl.pallas_call(
        functools.partial(_mm_body, n_pairs=len(pairs), has_res=res is not None, nk=nk),
        grid=(M // tm, n_out // tn, nk),
        in_specs=in_specs,
        out_specs=pl.BlockSpec((tm, tn), lambda i, j, k: (i, j)),
        out_shape=jax.ShapeDtypeStruct((M, n_out), out_dtype),
        scratch_shapes=scratch,
        compiler_params=_params("parallel", "parallel", "arbitrary"),
        name="matmul",
    )(*args)


def _swiglu_body(a_ref, wg_ref, wu_ref, o_ref):
    a = a_ref[...]
    g = jnp.dot(a, wg_ref[...], preferred_element_type=F32)
    u = jnp.dot(a, wu_ref[...], preferred_element_type=F32)
    o_ref[...] = (jax.nn.silu(g) * u).astype(o_ref.dtype)


def _swiglu_up(a, wg, wu, layer):
    M, K = a.shape
    N = wg.shape[2]
    tm = _tile(M, MM_TM)
    tn = _tile(N, MM_TN_F32)
    return pl.pallas_call(
        _swiglu_body,
        grid=(M // tm, N // tn),
        in_specs=[pl.BlockSpec((tm, K), lambda i, j: (i, 0)),
                  pl.BlockSpec((None, K, tn), lambda i, j: (layer, 0, j)),
                  pl.BlockSpec((None, K, tn), lambda i, j: (layer, 0, j))],
        out_specs=pl.BlockSpec((tm, tn), lambda i, j: (i, j)),
        out_shape=jax.ShapeDtypeStruct((M, N), BF16),
        compiler_params=_params("parallel", "parallel"),
        name="swiglu_up",
    )(a, wg, wu)


def _lru_body(x_ref, g_ref, cw_ref, cb_ref, wa_ref, ba_ref, wx_ref, bx_ref, lam_ref, o_ref,
              xpad_ref, a_scr, u_scr, h_scr, hstate_ref, *, ts, nblk):
    s = pl.program_id(2)

    @pl.when(s == 0)
    def _():
        xpad_ref[0:CONV_HIST, :] = jnp.zeros((CONV_HIST, xpad_ref.shape[1]), F32)
        hstate_ref[...] = jnp.zeros(hstate_ref.shape, F32)

    xpad_ref[CONV_HIST:CONV_HIST + ts, :] = x_ref[...]
    kw = cw_ref.shape[0]
    xc = cw_ref[kw - 1:kw, :] * x_ref[...]
    for k in range(kw - 1):
        off = CONV_HIST - (kw - 1) + k
        xc = xc + cw_ref[k:k + 1, :] * xpad_ref[off:off + ts, :]
    xc = xc + cb_ref[...]
    xpad_ref[0:CONV_HIST, :] = x_ref[ts - CONV_HIST:ts, :]

    lam = lam_ref[...]
    sp = jnp.maximum(-lam, 0.0) + jnp.log1p(jnp.exp(-jnp.abs(lam)))
    xcb = xc.astype(BF16)
    for n in range(nblk):
        sl = slice(n * LRU_BLOCK, (n + 1) * LRU_BLOCK)
        ga = jnp.dot(xcb[:, sl], wa_ref[n], preferred_element_type=F32) + ba_ref[:, sl]
        gx = jnp.dot(xcb[:, sl], wx_ref[n], preferred_element_type=F32) + bx_ref[:, sl]
        log_a = -LRU_C * jax.nn.sigmoid(ga) * sp[:, sl]
        th = jnp.tanh(log_a)
        mult = jnp.sqrt(-2.0 * th / (1.0 - th))
        a_scr[:, sl] = jnp.exp(log_a)
        u_scr[:, sl] = mult * jax.nn.sigmoid(gx) * xc[:, sl]

    def step(t, h):
        h = a_scr[pl.ds(t, 1), :] * h + u_scr[pl.ds(t, 1), :]
        h_scr[pl.ds(t, 1), :] = h
        return h

    hstate_ref[...] = lax.fori_loop(0, ts, step, hstate_ref[...], unroll=8)
    o_ref[...] = (h_scr[...] * jax.nn.gelu(g_ref[...])).astype(o_ref.dtype)


def _lru(u3, layer, d_lru, conv_w, conv_b, wa, ba, wx, bx, lam):
    B, S, _ = u3.shape
    ts = _tile(S, SEQ_TILE)
    cb = _tile(d_lru, LRU_CB)
    nblk = cb // LRU_BLOCK
    kw = conv_w.shape[1]
    gate_col0 = d_lru // cb
    vec = lambda: pl.BlockSpec((None, 1, cb), lambda b, c, s: (layer, 0, c))
    gate_w = lambda: pl.BlockSpec((None, nblk, LRU_BLOCK, LRU_BLOCK),
                                  lambda b, c, s: (layer, c, 0, 0))
    return pl.pallas_call(
        functools.partial(_lru_body, ts=ts, nblk=nblk),
        grid=(B, d_lru // cb, S // ts),
        in_specs=[pl.BlockSpec((None, ts, cb), lambda b, c, s: (b, s, c)),
                  pl.BlockSpec((None, ts, cb), lambda b, c, s: (b, s, gate_col0 + c)),
                  pl.BlockSpec((None, kw, cb), lambda b, c, s: (layer, 0, c)),
                  vec(), gate_w(), vec(), gate_w(), vec(), vec()],
        out_specs=pl.BlockSpec((None, ts, cb), lambda b, c, s: (b, s, c)),
        out_shape=jax.ShapeDtypeStruct((B, S, d_lru), BF16),
        scratch_shapes=[pltpu.VMEM((CONV_HIST + ts, cb), F32),
                        pltpu.VMEM((ts, cb), F32),
                        pltpu.VMEM((ts, cb), F32),
                        pltpu.VMEM((ts, cb), F32),
                        pltpu.VMEM((1, cb), F32)],
        compiler_params=_params("parallel", "parallel", "arbitrary"),
        name="lru",
    )(u3, u3, conv_w, conv_b, wa, ba, wx, bx, lam)


def _pool_body(x_ref, w_ref, sc_ref, o_ref, xpad_ref, *, ts, gw):
    s = pl.program_id(1)

    @pl.when(s == 0)
    def _():
        xpad_ref[0:POOL_HIST, :] = jnp.zeros((POOL_HIST, xpad_ref.shape[1]), F32)

    xpad_ref[POOL_HIST:POOL_HIST + ts, :] = x_ref[...]
    frames = (s * ts + 1 + lax.broadcasted_iota(jnp.int32, (ts, 1), 0)).astype(F32)
    for g, win in enumerate(POOL_WINDOWS):
        cs = slice(g * gw, (g + 1) * gw)
        acc = x_ref[:, cs]
        for k in range(1, win):
            acc = acc + xpad_ref[POOL_HIST - k:POOL_HIST - k + ts, cs]
        y = acc / jnp.minimum(frames, float(win)) - x_ref[:, cs]
        z = jnp.dot(y.astype(BF16), w_ref[g], preferred_element_type=F32) * sc_ref[:, cs]
        o_ref[:, cs] = z.astype(o_ref.dtype)
    xpad_ref[0:POOL_HIST, :] = x_ref[ts - POOL_HIST:ts, :]


def _pool(u3, layer, col_block, d_pool, w_pool, scale):
    B, S, _ = u3.shape
    ts = _tile(S, SEQ_TILE)
    ng = len(POOL_WINDOWS)
    gw = d_pool // ng
    assert max(POOL_WINDOWS) <= POOL_HIST <= ts
    return pl.pallas_call(
        functools.partial(_pool_body, ts=ts, gw=gw),
        grid=(B, S // ts),
        in_specs=[pl.BlockSpec((None, ts, d_pool), lambda b, s: (b, s, col_block)),
                  pl.BlockSpec((None, ng, gw, gw), lambda b, s: (layer, 0, 0, 0)),
                  pl.BlockSpec((None, 1, d_pool), lambda b, s: (layer, 0, 0))],
        out_specs=pl.BlockSpec((None, ts, d_pool), lambda b, s: (b, s, 0)),
        out_shape=jax.ShapeDtypeStruct((B, S, d_pool), BF16),
        scratch_shapes=[pltpu.VMEM((POOL_HIST + ts, d_pool), F32)],
        compiler_params=_params("parallel", "arbitrary"),
        name="pool",
    )(u3, w_pool, scale)


def _lanes(x, width):
    return jnp.concatenate([x] * (width // LANES), axis=1) if width > LANES else x


def _attn_tile(q_ref, k_ref, v_ref, m_scr, l_scr, acc_scr, *, masked, c_exp):
    v = v_ref[...]
    for c in range(2):
        cs = slice(c * HEAD_DIM, (c + 1) * HEAD_DIM)
        s = lax.dot_general(q_ref[:, cs], k_ref[:, cs], (((1,), (1,)), ((), ())),
                            preferred_element_type=F32)
        if masked:
            shift = CHUNK.bit_length() - 1
            qc = lax.shift_right_logical(lax.broadcasted_iota(jnp.int32, s.shape, 0), shift)
            kc = lax.shift_right_logical(lax.broadcasted_iota(jnp.int32, s.shape, 1), shift)
            s = jnp.where(kc <= qc, s, NEG_INF)
        m_prev = m_scr[c]
        m_new = jnp.maximum(m_prev, jnp.max(s, axis=1, keepdims=True))
        alpha = jnp.exp2((m_prev - m_new) * c_exp)
        p = jnp.exp2((s - _lanes(m_new, s.shape[1])) * c_exp)
        l_scr[c] = alpha * l_scr[c] + jnp.sum(p, axis=1, keepdims=True)
        acc_scr[c] = (_lanes(alpha, acc_scr.shape[2]) * acc_scr[c]
                      + jnp.dot(p.astype(BF16), v, preferred_element_type=F32))
        m_scr[c] = m_new


def _attn_body(qi_ref, kj_ref, lam_ref, g_ref, q_ref, k_ref, v_ref, o_ref, m_scr, l_scr, acc_scr,
               *, c_exp, lambda_init):
    step = pl.program_id(2)
    i = qi_ref[step]
    j = kj_ref[step]

    @pl.when(j == 0)
    def _():
        m_scr[...] = jnp.full(m_scr.shape, NEG_INF, F32)
        l_scr[...] = jnp.zeros(l_scr.shape, F32)
        acc_scr[...] = jnp.zeros(acc_scr.shape, F32)

    tile = functools.partial(_attn_tile, q_ref, k_ref, v_ref, m_scr, l_scr, acc_scr, c_exp=c_exp)

    @pl.when(j < i)
    def _():
        tile(masked=False)

    @pl.when(j == i)
    def _():
        tile(masked=True)
        lv = lam_ref[...]
        lam = (jnp.exp(jnp.sum(lv[0:1] * lv[1:2], axis=1, keepdims=True))
               - jnp.exp(jnp.sum(lv[2:3] * lv[3:4], axis=1, keepdims=True)) + lambda_init)
        hw = acc_scr.shape[2]
        o = acc_scr[0] / _lanes(l_scr[0], hw) - lam * (acc_scr[1] / _lanes(l_scr[1], hw))
        o_ref[...] = (_rms(o, g_ref[...]) * (1.0 - lambda_init)).astype(o_ref.dtype)


def _attention(qkv3, layer, lam_vecs, subln_g, lambda_init):
    B, S, d3 = qkv3.shape
    D = d3 // 3
    hw = 2 * HEAD_DIM
    H = D // hw
    t = _tile(S, ATTN_TILE)
    assert CHUNK & (CHUNK - 1) == 0 and t % CHUNK == 0
    c_exp = HEAD_DIM ** -0.5 * math.log2(math.e)
    pairs = [(i, j) for i in range(S // t) for j in range(i + 1)]
    qi = jnp.asarray(np.array([p[0] for p in pairs], np.int32))
    kj = jnp.asarray(np.array([p[1] for p in pairs], np.int32))
    return pl.pallas_call(
        functools.partial(_attn_body, c_exp=c_exp, lambda_init=lambda_init),
        grid_spec=pltpu.PrefetchScalarGridSpec(
            num_scalar_prefetch=2,
            grid=(B, H, len(pairs)),
            in_specs=[pl.BlockSpec((None, 4, HEAD_DIM), lambda b, h, s, qi, kj: (layer, 0, 0)),
                      pl.BlockSpec((None, 1, hw), lambda b, h, s, qi, kj: (layer, 0, 0)),
                      pl.BlockSpec((None, t, hw), lambda b, h, s, qi, kj: (b, qi[s], h)),
                      pl.BlockSpec((None, t, hw), lambda b, h, s, qi, kj: (b, kj[s], H + h)),
                      pl.BlockSpec((None, t, hw), lambda b, h, s, qi, kj: (b, kj[s], 2 * H + h))],
            out_specs=pl.BlockSpec((None, t, hw), lambda b, h, s, qi, kj: (b, qi[s], h)),
            scratch_shapes=[pltpu.VMEM((2, t, LANES), F32),
                            pltpu.VMEM((2, t, LANES), F32),
                            pltpu.VMEM((2, t, hw), F32)]),
        out_shape=jax.ShapeDtypeStruct((B, S, D), BF16),
        compiler_params=_params("parallel", "parallel", "arbitrary"),
        name="diff_attention",
    )(qi, kj, lam_vecs, subln_g, qkv3, qkv3, qkv3)


ROUTE_IDX0, ROUTE_IDX1, ROUTE_W0, ROUTE_W1 = 0, 1, 2, 3


def _pack_halves(x):
    n = x.shape[1] // 2
    bits = lax.bitcast_convert_type(x.astype(jnp.bfloat16).astype(F32), jnp.uint32)
    return bits[:, n:] | lax.shift_right_logical(bits[:, :n], jnp.uint32(16))


def _unpack_halves(w):
    lo = lax.bitcast_convert_type(lax.shift_left(w, jnp.uint32(16)), F32)
    hi = lax.bitcast_convert_type(w & jnp.uint32(0xFFFF0000), F32)
    return lo, hi


def _router_body(x_ref, g_ref, wr_ref, h_ref, route_ref, *, n_experts):
    h = _rms(x_ref[...], g_ref[...])
    h_ref[...] = _pack_halves(h)
    logits = jnp.dot(h, wr_ref[...], preferred_element_type=F32,
                     precision=lax.Precision.HIGHEST)
    col = lax.broadcasted_iota(jnp.int32, logits.shape, 1)
    logits = jnp.where(col < n_experts, logits, -jnp.inf)
    v0 = jnp.max(logits, axis=1, keepdims=True)
    i0 = jnp.min(jnp.where(logits == v0, col, LANES), axis=1, keepdims=True)
    rest = jnp.where(col == i0, -jnp.inf, logits)
    v1 = jnp.max(rest, axis=1, keepdims=True)
    i1 = jnp.min(jnp.where(rest == v1, col, LANES), axis=1, keepdims=True)
    e = jnp.exp(v1 - v0)
    w0 = 1.0 / (1.0 + e)
    w1 = e / (1.0 + e)
    route = jnp.where(col == ROUTE_IDX0, i0.astype(F32), 0.0)
    route = jnp.where(col == ROUTE_IDX1, i1.astype(F32), route)
    route = jnp.where(col == ROUTE_W0, w0, route)
    route = jnp.where(col == ROUTE_W1, w1, route)
    route_ref[...] = route


def _norm_router(x, g, w_router_padded, n_experts):
    T, D = x.shape
    tr = _tile(T, ROW_TILE)
    return pl.pallas_call(
        functools.partial(_router_body, n_experts=n_experts),
        grid=(T // tr,),
        in_specs=[pl.BlockSpec((tr, D), lambda i: (i, 0)),
                  pl.BlockSpec((1, D), lambda i: (0, 0)),
                  pl.BlockSpec((D, LANES), lambda i: (0, 0))],
        out_specs=[pl.BlockSpec((tr, D // 2), lambda i: (i, 0)),
                   pl.BlockSpec((tr, LANES), lambda i: (i, 0))],
        out_shape=[jax.ShapeDtypeStruct((T, D // 2), jnp.uint32),
                   jax.ShapeDtypeStruct((T, LANES), F32)],
        compiler_params=_params("parallel"),
        name="norm_router",
    )(x, g.reshape(1, D), w_router_padded)


def _slot_plan(expert_idx, n_experts, tm):
    T = expert_idx.shape[0]
    flat = expert_idx.reshape(T * TOP_K)
    onehot = (flat[:, None] == jnp.arange(n_experts, dtype=jnp.int32)[None, :]).astype(jnp.int32)
    csum = jnp.cumsum(onehot, axis=0)
    pos = jnp.sum(csum * onehot, axis=1) - 1
    counts = csum[-1]
    padded = (counts + tm - 1) // tm * tm
    gend = jnp.cumsum(padded)
    gstart = gend - padded
    slot = (jnp.sum(onehot * gstart[None, :], axis=1) + pos).astype(jnp.int32)
    n_tiles = (T * TOP_K) // tm + n_experts
    tile_start = jnp.arange(n_tiles, dtype=jnp.int32) * tm
    tile_expert = jnp.sum((tile_start[:, None] >= gend[None, :]).astype(jnp.int32), axis=1)
    n_valid = (gend[-1] // tm).astype(jnp.int32).reshape(1)
    tile_expert = jnp.minimum(tile_expert, n_experts - 1).astype(jnp.int32)
    token = jnp.arange(T * TOP_K, dtype=jnp.int32) // TOP_K
    src = jnp.zeros((n_tiles * tm,), jnp.int32).at[slot].set(token)
    return slot, src, tile_expert, n_valid, n_tiles


def _row_copies(index, n_rows, src_ref, dst_ref, sem):
    def copy(r):
        return pltpu.make_async_copy(src_ref.at[pl.ds(index(r), 1), :],
                                     dst_ref.at[pl.ds(r, 1), :], sem)

    def start_one(r, carry):
        copy(r).start()
        return carry

    def wait_one(r, carry):
        copy(r).wait()
        return carry

    start = lambda: lax.fori_loop(0, n_rows, start_one, 0, unroll=8)
    wait = lambda: lax.fori_loop(0, n_rows, wait_one, 0, unroll=8)
    return start, wait


def _gather_slots_body(src_ref, nv_ref, h_ref, o_ref, sem, *, tm):
    i = pl.program_id(0)
    valid = i < nv_ref[0]

    @pl.when(valid)
    def _():
        start, wait = _row_copies(lambda r: src_ref[i * tm + r], tm, h_ref, o_ref, sem)
        start()
        wait()

    @pl.when(jnp.logical_not(valid))
    def _():
        o_ref[...] = jnp.zeros(o_ref.shape, o_ref.dtype)


def _gather_slots(src, n_valid, hp, n_tiles, tm):
    _, W = hp.shape
    return pl.pallas_call(
        functools.partial(_gather_slots_body, tm=tm),
        grid_spec=pltpu.PrefetchScalarGridSpec(
            num_scalar_prefetch=2,
            grid=(n_tiles,),
            in_specs=[pl.BlockSpec(memory_space=pl.ANY)],
            out_specs=pl.BlockSpec((tm, W), lambda i, src, nv: (i, 0)),
            scratch_shapes=[pltpu.SemaphoreType.DMA(())]),
        out_shape=jax.ShapeDtypeStruct((n_tiles * tm, W), hp.dtype),
        compiler_params=_params("arbitrary"),
        name="gather_slots",
    )(src, n_valid, hp)


def _moe_up_body(a_ref, wg_ref, wu_ref, o_ref):
    lo, hi = _unpack_halves(a_ref[...])
    a = jnp.concatenate([lo.astype(BF16), hi.astype(BF16)], axis=1)
    g = jnp.dot(a, wg_ref[...], preferred_element_type=F32)
    u = jnp.dot(a, wu_ref[...], preferred_element_type=F32)
    o_ref[...] = (jax.nn.silu(g) * u).astype(o_ref.dtype)


def _moe_down_body(a_ref, w_ref, o_ref):
    o_ref[...] = _pack_halves(jnp.dot(a_ref[...], w_ref[...], preferred_element_type=F32))


def _grouped_body(te_ref, nv_ref, *refs, body):
    del te_ref
    valid = pl.program_id(1) < nv_ref[0]
    o_ref = refs[-1]

    @pl.when(valid)
    def _():
        body(*refs)

    @pl.when(jnp.logical_not(valid))
    def _():
        o_ref[...] = jnp.zeros(o_ref.shape, o_ref.dtype)


def _grouped_call(body, tile_expert, n_valid, a, weights, layer, n_tiles, tm, name, *, packed_out):
    P, ka = a.shape
    K, N = weights[0].shape[2:]
    tn = _tile(N, MM_TN)
    pack = 2 if packed_out else 1
    row = lambda n, i, te, nv: jnp.maximum(jnp.minimum(i, nv[0] - 1), 0)
    w_spec = pl.BlockSpec((None, None, K, tn),
                          lambda n, i, te, nv: (layer, te[row(n, i, te, nv)], 0, n))
    return pl.pallas_call(
        functools.partial(_grouped_body, body=body),
        grid_spec=pltpu.PrefetchScalarGridSpec(
            num_scalar_prefetch=2,
            grid=(N // tn, n_tiles),
            in_specs=[pl.BlockSpec((tm, ka), lambda n, i, te, nv: (row(n, i, te, nv), 0))]
            + [w_spec] * len(weights),
            out_specs=pl.BlockSpec((tm, tn // pack), lambda n, i, te, nv: (i, n))),
        out_shape=jax.ShapeDtypeStruct((P, N // pack), jnp.uint32 if packed_out else BF16),
        compiler_params=_params("parallel", "arbitrary"),
        name=name,
    )(tile_expert, n_valid, a, *weights)


def _combine_body(slot_ref, x_ref, route_ref, y_ref, g_ref, *rest, emit_x, tn):
    out_refs, (ybuf, sem) = rest[:-2], rest[-2:]
    tr = x_ref.shape[0]
    base = pl.program_id(0) * tr
    loops = [_row_copies(lambda r, k=k: slot_ref[TOP_K * (base + r) + k], tr, y_ref, ybuf.at[k], sem)
             for k in range(TOP_K)]
    for start, _ in loops:
        start()
    for _, wait in loops:
        wait()
    half = tn // 2
    y = None
    for k, col in enumerate((ROUTE_W0, ROUTE_W1)):
        lo, hi = _unpack_halves(ybuf[k])
        parts = []
        for n in range(lo.shape[1] // half):
            parts += [lo[:, n * half:(n + 1) * half], hi[:, n * half:(n + 1) * half]]
        yk = route_ref[:, col:col + 1] * jnp.concatenate(parts, axis=1)
        y = yk if y is None else y + yk
    x = x_ref[...] + y
    if emit_x:
        out_refs[0][...] = x
    h_ref = out_refs[-1]
    h_ref[...] = _rms(x, g_ref[...]).astype(h_ref.dtype)


def _combine_norm(slot, x, route, yp, g, tn, *, emit_x, h_dtype):
    T, D = x.shape
    tr = _tile(T, ROW_TILE)
    row = pl.BlockSpec((tr, D), lambda i, slot: (i, 0))
    outs = ([jax.ShapeDtypeStruct((T, D), F32)] if emit_x else []) + \
        [jax.ShapeDtypeStruct((T, D), h_dtype)]
    return pl.pallas_call(
        functools.partial(_combine_body, emit_x=emit_x, tn=tn),
        grid_spec=pltpu.PrefetchScalarGridSpec(
            num_scalar_prefetch=1,
            grid=(T // tr,),
            in_specs=[row,
                      pl.BlockSpec((tr, LANES), lambda i, slot: (i, 0)),
                      pl.BlockSpec(memory_space=pl.ANY),
                      pl.BlockSpec((1, D), lambda i, slot: (0, 0))],
            out_specs=[row] * len(outs),
            scratch_shapes=[pltpu.VMEM((TOP_K, tr, D // 2), jnp.uint32),
                            pltpu.SemaphoreType.DMA(())]),
        out_shape=outs,
        compiler_params=_params("arbitrary"),
        name="combine_norm",
    )(slot, x, route, yp, g.reshape(1, D))


def _moe(x, norm_g, w_router, w_gate, w_up, w_down, layer, next_g, *, emit_x, h_dtype):
    T, D = x.shape
    n_experts = w_router.shape[2]
    tm = _tile(T * TOP_K // n_experts, MOE_TM)
    wr = jnp.pad(w_router[layer], ((0, 0), (0, LANES - n_experts)))
    hp, route = _norm_router(x, norm_g, wr, n_experts)
    expert_idx = route[:, ROUTE_IDX0:ROUTE_IDX1 + 1].astype(jnp.int32)
    slot, src, tile_expert, n_valid, n_tiles = _slot_plan(expert_idx, n_experts, tm)
    xg = _gather_slots(src, n_valid, hp, n_tiles, tm)
    act = _grouped_call(_moe_up_body, tile_expert, n_valid, xg, [w_gate, w_up], layer,
                        n_tiles, tm, "moe_up", packed_out=False)
    yp = _grouped_call(_moe_down_body, tile_expert, n_valid, act, [w_down], layer,
                       n_tiles, tm, "moe_down", packed_out=True)
    return _combine_norm(slot, x, route, yp, next_g, _tile(D, MM_TN), emit_x=emit_x, h_dtype=h_dtype)


def kernel(x, ev_norm1, ev_w_in, ev_conv_w, ev_conv_b, ev_gate_a_w, ev_gate_a_b, ev_gate_x_w, ev_gate_x_b, ev_lru_lambda, ev_pool_w, ev_pool_scale, ev_w_out, ev_norm2, ev_ffn_gate, ev_ffn_up, ev_ffn_down, od_norm1, od_w_qkv, od_lambda, od_subln, od_w_o, od_norm2, od_router, od_exp_gate, od_exp_up, od_exp_down, final_norm):
    B, S, D = x.shape
    T = B * S
    n_even, n_odd = ev_norm1.shape[0], od_norm1.shape[0]
    depth = n_even + n_odd
    d_lru = ev_lru_lambda.shape[1]
    d_pool = ev_pool_scale.shape[1]

    w_in, w_out = ev_w_in.astype(BF16), ev_w_out.astype(BF16)
    gate_a_w, gate_x_w = ev_gate_a_w.astype(BF16), ev_gate_x_w.astype(BF16)
    pool_w = ev_pool_w.astype(BF16)
    ffn_gate, ffn_up, ffn_down = (w.astype(BF16) for w in (ev_ffn_gate, ev_ffn_up, ev_ffn_down))
    w_qkv, w_o = od_w_qkv.astype(BF16), od_w_o.astype(BF16)
    exp_gate, exp_up, exp_down = (w.astype(BF16) for w in (od_exp_gate, od_exp_up, od_exp_down))
    row3 = lambda v: v.reshape(v.shape[0], 1, v.shape[1])

    xs = x.reshape(T, D)
    h = None
    out = None
    for layer in range(depth):
        j = layer // 2
        if layer % 2 == 0:
            if h is None:
                h = _rmsnorm(xs, ev_norm1[j], BF16)
            u3 = _matmul([(h, w_in, 0)], j, w_in.shape[2], out_dtype=F32).reshape(B, S, -1)
            y_lru = _lru(u3, j, d_lru, ev_conv_w, row3(ev_conv_b), gate_a_w, row3(ev_gate_a_b),
                         gate_x_w, row3(ev_gate_x_b), row3(ev_lru_lambda))
            assert (2 * d_lru) % d_pool == 0
            y_pool = _pool(u3, j, 2 * d_lru // d_pool, d_pool, pool_w, row3(ev_pool_scale))
            assert d_lru == d_pool
            xs = _matmul([(y_lru.reshape(T, d_lru), w_out, 0), (y_pool.reshape(T, d_pool), w_out, 1)],
                         j, D, res=xs, out_dtype=F32)
            h = _rmsnorm(xs, ev_norm2[j], BF16)
            act = _swiglu_up(h, ffn_gate, ffn_up, j)
            xs = _matmul([(act, ffn_down, 0)], j, D, res=xs, out_dtype=F32, split_k=True)
            h = None
        else:
            lambda_init = 0.8 - 0.6 * math.exp(-0.3 * layer)
            if h is None:
                h = _rmsnorm(xs, od_norm1[j], BF16)
            qkv3 = _matmul([(h, w_qkv, 0)], j, 3 * D, out_dtype=BF16).reshape(B, S, 3 * D)
            o = _attention(qkv3, j, od_lambda, row3(od_subln), lambda_init)
            xs = _matmul([(o.reshape(T, D), w_o, 0)], j, D, res=xs, out_dtype=F32)
            last = layer == depth - 1
            if last:
                next_g = final_norm
            else:
                next_g = ev_norm1[j + 1]
            res = _moe(xs, od_norm2[j], od_router, exp_gate, exp_up, exp_down, j, next_g,
                       emit_x=not last, h_dtype=F32 if last else BF16)
            if last:
                out = res[0]
            else:
                xs, h = res
    if out is None:
        out = _rmsnorm(xs, final_norm, F32)
    return out.reshape(B, S, D)
```

```python
import functools
import math

import jax
import jax.numpy as jnp
import numpy as np
from jax import lax
from jax.experimental import pallas as pl
from jax.experimental.pallas import tpu as pltpu

F32 = jnp.float32
BF16 = jnp.bfloat16

EPS = 1e-6
NEG_INF = -1e30
CHUNK = 64
LRU_C = 8.0
LRU_BLOCK = 128
POOL_WINDOWS = (2, 4, 8, 16)
HEAD_DIM = 128
TOP_K = 2

LANES = 128
SUBLANES = 8
VMEM_PHYSICAL_BYTES = 64 * 1024 * 1024
VMEM_LIMIT_BYTES = VMEM_PHYSICAL_BYTES - 8 * 1024 * 1024

CONV_HIST = SUBLANES
POOL_HIST = 2 * SUBLANES

ROW_TILE = 256
MM_TM = 1024
MM_TM_NORM = 512
MM_TN = 1024
MM_TN_F32 = 512
MM_TK = 2048
SEQ_TILE = 512
LRU_CB = 2048
ATTN_TILE = 512
ATTN_HEADS = 4
MOE_TM = 512
DMA_ROWS = 256


def _tile(dim, pref):
    t = min(dim, pref)
    assert dim % t == 0, (dim, pref)
    return t


def _params(*semantics):
    return pltpu.CompilerParams(dimension_semantics=semantics,
                                vmem_limit_bytes=VMEM_LIMIT_BYTES)


def _rms(x, g):
    return x * lax.rsqrt(jnp.mean(x * x, axis=-1, keepdims=True) + EPS) * g


def _rmsnorm_body(x_ref, g_ref, o_ref):
    o_ref[...] = _rms(x_ref[...], g_ref[...]).astype(o_ref.dtype)


def _rmsnorm(x, g, out_dtype):
    T, D = x.shape
    tr = _tile(T, ROW_TILE)
    return pl.pallas_call(
        _rmsnorm_body,
        grid=(T // tr,),
        in_specs=[pl.BlockSpec((tr, D), lambda i: (i, 0)),
                  pl.BlockSpec((1, D), lambda i: (0, 0))],
        out_specs=pl.BlockSpec((tr, D), lambda i: (i, 0)),
        out_shape=jax.ShapeDtypeStruct((T, D), out_dtype),
        compiler_params=_params("parallel"),
        name="rmsnorm",
    )(x, g.reshape(1, D))


def _mm_body(*refs, n_pairs, has_res, nk, lead_scale):
    pairs = [(refs[2 * p], refs[2 * p + 1]) for p in range(n_pairs)]
    pos = 2 * n_pairs
    res_ref = refs[pos] if has_res else None
    o_ref = refs[pos + has_res]
    part = None
    for a_ref, w_ref in pairs:
        d = jnp.dot(a_ref[...], w_ref[...], preferred_element_type=F32)
        part = d if part is None else part + d

    def finish(total):
        if lead_scale is not None:
            scale, n_tiles = lead_scale
            total = total * jnp.where(pl.program_id(1) < n_tiles, scale, 1.0).astype(F32)
        if has_res:
            total = res_ref[...] + total
        o_ref[...] = total.astype(o_ref.dtype)

    if nk == 1:
        finish(part)
    else:
        acc_ref = refs[pos + has_res + 1]
        k = pl.program_id(2)

        @pl.when(k == 0)
        def _():
            acc_ref[...] = part

        @pl.when(k > 0)
        def _():
            acc_ref[...] += part

        @pl.when(k == nk - 1)
        def _():
            finish(acc_ref[...])


def _matmul(pairs, layer, n_out, *, res=None, out_dtype, split_k=False, lead_cols_scale=None):
    M = pairs[0][0].shape[0]
    tm = _tile(M, MM_TM)
    tn = _tile(n_out, MM_TN if res is None else MM_TN_F32)
    lead_scale = None
    if lead_cols_scale is not None:
        assert not split_k and lead_cols_scale[1] % tn == 0
        lead_scale = (lead_cols_scale[0], lead_cols_scale[1] // tn)
    if split_k:
        assert len(pairs) == 1
        tk = _tile(pairs[0][0].shape[1], MM_TK)
        tn = _tile(n_out, MM_TN)
        nk = pairs[0][0].shape[1] // tk
    else:
        nk = 1
    in_specs, args = [], []
    for a, w, rb in pairs:
        ka = a.shape[1]
        if nk == 1:
            in_specs.append(pl.BlockSpec((tm, ka), lambda i, j, k: (i, 0)))
            in_specs.append(pl.BlockSpec((None, ka, tn),
                                         lambda i, j, k, rb=rb: (layer, rb, j)))
        else:
            in_specs.append(pl.BlockSpec((tm, tk), lambda i, j, k: (i, k)))
            in_specs.append(pl.BlockSpec((None, tk, tn), lambda i, j, k: (layer, k, j)))
        args += [a, w]
    if res is not None:
        in_specs.append(pl.BlockSpec((tm, tn), lambda i, j, k: (i, j)))
        args.append(res)
    scratch = [pltpu.VMEM((tm, tn), F32)] if nk > 1 else []
    return pl.pallas_call(
        functools.partial(_mm_body, n_pairs=len(pairs), has_res=res is not None, nk=nk,
                          lead_scale=lead_scale),
        grid=(M // tm, n_out // tn, nk),
        in_specs=in_specs,
        out_specs=pl.BlockSpec((tm, tn), lambda i, j, k: (i, j)),
        out_shape=jax.ShapeDtypeStruct((M, n_out), out_dtype),
        scratch_shapes=scratch,
        compiler_params=_params("parallel", "parallel", "arbitrary"),
        name="matmul",
    )(*args)


def _norm_mm_body(x_ref, g_ref, *refs, swiglu, lead_scale):
    w_refs, o_ref, hn_ref = refs[:-2], refs[-2], refs[-1]

    @pl.when(pl.program_id(1) == 0)
    def _():
        hn_ref[...] = _rms(x_ref[...], g_ref[...]).astype(hn_ref.dtype)

    a = hn_ref[...]
    out = jnp.dot(a, w_refs[0][...], preferred_element_type=F32)
    if swiglu:
        out = jax.nn.silu(out) * jnp.dot(a, w_refs[1][...], preferred_element_type=F32)
    if lead_scale is not None:
        scale, n_tiles = lead_scale
        out = out * jnp.where(pl.program_id(1) < n_tiles, scale, 1.0).astype(F32)
    o_ref[...] = out.astype(o_ref.dtype)


def _norm_matmul(x, gains, weights, layer, *, out_dtype, swiglu=False, lead_cols_scale=None):
    M, K = x.shape
    N = weights[0].shape[2]
    tm = _tile(M, MM_TM_NORM)
    tn = _tile(N, MM_TN if (out_dtype == BF16 and not swiglu) else MM_TN_F32)
    lead_scale = None
    if lead_cols_scale is not None:
        assert lead_cols_scale[1] % tn == 0
        lead_scale = (lead_cols_scale[0], lead_cols_scale[1] // tn)
    return pl.pallas_call(
        functools.partial(_norm_mm_body, swiglu=swiglu, lead_scale=lead_scale),
        grid=(M // tm, N // tn),
        in_specs=[pl.BlockSpec((tm, K), lambda i, j: (i, 0)),
                  pl.BlockSpec((None, 1, K), lambda i, j: (layer, 0, 0))]
        + [pl.BlockSpec((None, K, tn), lambda i, j: (layer, 0, j))] * len(weights),
        out_specs=pl.BlockSpec((tm, tn), lambda i, j: (i, j)),
        out_shape=jax.ShapeDtypeStruct((M, N), out_dtype),
        scratch_shapes=[pltpu.VMEM((tm, K), BF16)],
        compiler_params=_params("parallel", "arbitrary"),
        name="norm_swiglu_up" if swiglu else "norm_matmul",
    )(x, gains.reshape(gains.shape[0], 1, K), *weights)


def _lru_body(x_ref, g_ref, cw_ref, cb_ref, wa_ref, ba_ref, wx_ref, bx_ref, lam_ref, o_ref,
              xpad_ref, a_scr, u_scr, h_scr, hstate_ref, *, ts, nblk):
    s = pl.program_id(2)

    @pl.when(s == 0)
    def _():
        xpad_ref[0:CONV_HIST, :] = jnp.zeros((CONV_HIST, xpad_ref.shape[1]), F32)
        hstate_ref[...] = jnp.zeros(hstate_ref.shape, F32)

    xpad_ref[CONV_HIST:CONV_HIST + ts, :] = x_ref[...]
    kw = cw_ref.shape[0]
    xc = cw_ref[kw - 1:kw, :] * x_ref[...]
    for k in range(kw - 1):
        off = CONV_HIST - (kw - 1) + k
        xc = xc + cw_ref[k:k + 1, :] * xpad_ref[off:off + ts, :]
    xc = xc + cb_ref[...]
    xpad_ref[0:CONV_HIST, :] = x_ref[ts - CONV_HIST:ts, :]

    lam = lam_ref[...]
    sp = jnp.maximum(-lam, 0.0) + jnp.log1p(jnp.exp(-jnp.abs(lam)))
    xcb = xc.astype(BF16)
    for n in range(nblk):
        sl = slice(n * LRU_BLOCK, (n + 1) * LRU_BLOCK)
        ga = jnp.dot(xcb[:, sl], wa_ref[n], preferred_element_type=F32) + ba_ref[:, sl]
        gx = jnp.dot(xcb[:, sl], wx_ref[n], preferred_element_type=F32) + bx_ref[:, sl]
        log_a = -LRU_C * jax.nn.sigmoid(ga) * sp[:, sl]
        th = jnp.tanh(log_a)
        mult = jnp.sqrt(-2.0 * th / (1.0 - th))
        a_scr[:, sl] = jnp.exp(log_a)
        u_scr[:, sl] = mult * jax.nn.sigmoid(gx) * xc[:, sl]

    def step(t, h):
        h = a_scr[pl.ds(t, 1), :] * h + u_scr[pl.ds(t, 1), :]
        h_scr[pl.ds(t, 1), :] = h
        return h

    hstate_ref[...] = lax.fori_loop(0, ts, step, hstate_ref[...], unroll=8)
    o_ref[...] = (h_scr[...] * jax.nn.gelu(g_ref[...])).astype(o_ref.dtype)


def _lru(u3, layer, d_lru, conv_w, conv_b, wa, ba, wx, bx, lam):
    B, S, _ = u3.shape
    ts = _tile(S, SEQ_TILE)
    cb = _tile(d_lru, LRU_CB)
    nblk = cb // LRU_BLOCK
    kw = conv_w.shape[1]
    gate_col0 = d_lru // cb
    vec = lambda: pl.BlockSpec((None, 1, cb), lambda b, c, s: (layer, 0, c))
    gate_w = lambda: pl.BlockSpec((None, nblk, LRU_BLOCK, LRU_BLOCK),
                                  lambda b, c, s: (layer, c, 0, 0))
    return pl.pallas_call(
        functools.partial(_lru_body, ts=ts, nblk=nblk),
        grid=(B, d_lru // cb, S // ts),
        in_specs=[pl.BlockSpec((None, ts, cb), lambda b, c, s: (b, s, c)),
                  pl.BlockSpec((None, ts, cb), lambda b, c, s: (b, s, gate_col0 + c)),
                  pl.BlockSpec((None, kw, cb), lambda b, c, s: (layer, 0, c)),
                  vec(), gate_w(), vec(), gate_w(), vec(), vec()],
        out_specs=pl.BlockSpec((None, ts, cb), lambda b, c, s: (b, s, c)),
        out_shape=jax.ShapeDtypeStruct((B, S, d_lru), BF16),
        scratch_shapes=[pltpu.VMEM((CONV_HIST + ts, cb), F32),
                        pltpu.VMEM((ts, cb), F32),
                        pltpu.VMEM((ts, cb), F32),
                        pltpu.VMEM((ts, cb), F32),
                        pltpu.VMEM((1, cb), F32)],
        compiler_params=_params("parallel", "parallel", "arbitrary"),
        name="lru",
    )(u3, u3, conv_w, conv_b, wa, ba, wx, bx, lam)


def _pool_body(x_ref, w_ref, sc_ref, o_ref, xpad_ref, *, ts, gw):
    s = pl.program_id(1)

    @pl.when(s == 0)
    def _():
        xpad_ref[0:POOL_HIST, :] = jnp.zeros((POOL_HIST, xpad_ref.shape[1]), F32)

    xpad_ref[POOL_HIST:POOL_HIST + ts, :] = x_ref[...]
    frames = (s * ts + 1 + lax.broadcasted_iota(jnp.int32, (ts, 1), 0)).astype(F32)
    for g, win in enumerate(POOL_WINDOWS):
        cs = slice(g * gw, (g + 1) * gw)
        acc = x_ref[:, cs]
        for k in range(1, win):
            acc = acc + xpad_ref[POOL_HIST - k:POOL_HIST - k + ts, cs]
        y = acc / jnp.minimum(frames, float(win)) - x_ref[:, cs]
        z = jnp.dot(y.astype(BF16), w_ref[g], preferred_element_type=F32) * sc_ref[:, cs]
        o_ref[:, cs] = z.astype(o_ref.dtype)
    xpad_ref[0:POOL_HIST, :] = x_ref[ts - POOL_HIST:ts, :]


def _pool(u3, layer, col_block, d_pool, w_pool, scale):
    B, S, _ = u3.shape
    ts = _tile(S, SEQ_TILE)
    ng = len(POOL_WINDOWS)
    gw = d_pool // ng
    assert max(POOL_WINDOWS) <= POOL_HIST <= ts
    return pl.pallas_call(
        functools.partial(_pool_body, ts=ts, gw=gw),
        grid=(B, S // ts),
        in_specs=[pl.BlockSpec((None, ts, d_pool), lambda b, s: (b, s, col_block)),
                  pl.BlockSpec((None, ng, gw, gw), lambda b, s: (layer, 0, 0, 0)),
                  pl.BlockSpec((None, 1, d_pool), lambda b, s: (layer, 0, 0))],
        out_specs=pl.BlockSpec((None, ts, d_pool), lambda b, s: (b, s, 0)),
        out_shape=jax.ShapeDtypeStruct((B, S, d_pool), BF16),
        scratch_shapes=[pltpu.VMEM((POOL_HIST + ts, d_pool), F32)],
        compiler_params=_params("parallel", "arbitrary"),
        name="pool",
    )(u3, w_pool, scale)


def _lanes(x, width):
    return jnp.concatenate([x] * (width // LANES), axis=1) if width > LANES else x


def _attn_tile(q_ref, k_ref, v_ref, m_scr, l_scr, acc_scr, *, masked):
    hw = acc_scr.shape[2]
    for r in range(m_scr.shape[0]):
        head, c = divmod(r, 2)
        cs = slice(head * hw + c * HEAD_DIM, head * hw + (c + 1) * HEAD_DIM)
        s = lax.dot_general(q_ref[:, cs], k_ref[:, cs], (((1,), (1,)), ((), ())),
                            preferred_element_type=F32)
        if masked:
            shift = CHUNK.bit_length() - 1
            qc = lax.shift_right_logical(lax.broadcasted_iota(jnp.int32, s.shape, 0), shift)
            kc = lax.shift_right_logical(lax.broadcasted_iota(jnp.int32, s.shape, 1), shift)
            s = jnp.where(kc <= qc, s, NEG_INF)
        m_prev = m_scr[r]
        m_new = jnp.maximum(m_prev, jnp.max(s, axis=1, keepdims=True))
        alpha = jnp.exp2(m_prev - m_new)
        p = jnp.exp2(s - _lanes(m_new, s.shape[1]))
        l_scr[r] = alpha * l_scr[r] + jnp.sum(p, axis=1, keepdims=True)
        acc_scr[r] = (_lanes(alpha, hw) * acc_scr[r]
                      + jnp.dot(p.astype(BF16), v_ref[:, head * hw:(head + 1) * hw],
                                preferred_element_type=F32))
        m_scr[r] = m_new


def _attn_body(qi_ref, kj_ref, lam_ref, g_ref, q_ref, k_ref, v_ref, o_ref, m_scr, l_scr, acc_scr,
               *, lambda_init):
    step = pl.program_id(2)
    i = qi_ref[step]
    j = kj_ref[step]

    @pl.when(j == 0)
    def _():
        m_scr[...] = jnp.full(m_scr.shape, NEG_INF, F32)
        l_scr[...] = jnp.zeros(l_scr.shape, F32)
        acc_scr[...] = jnp.zeros(acc_scr.shape, F32)

    tile = functools.partial(_attn_tile, q_ref, k_ref, v_ref, m_scr, l_scr, acc_scr)

    @pl.when(j < i)
    def _():
        tile(masked=False)

    @pl.when(j == i)
    def _():
        tile(masked=True)
        lv = lam_ref[...]
        lam = (jnp.exp(jnp.sum(lv[0:1] * lv[1:2], axis=1, keepdims=True))
               - jnp.exp(jnp.sum(lv[2:3] * lv[3:4], axis=1, keepdims=True)) + lambda_init)
        hw = acc_scr.shape[2]
        for head in range(m_scr.shape[0] // 2):
            r = 2 * head
            o = (acc_scr[r] / _lanes(l_scr[r], hw)
                 - lam * (acc_scr[r + 1] / _lanes(l_scr[r + 1], hw)))
            o_ref[:, head * hw:(head + 1) * hw] = (
                _rms(o, g_ref[...]) * (1.0 - lambda_init)).astype(o_ref.dtype)


def _attention(qkv3, layer, lam_vecs, subln_g, lambda_init):
    B, S, d3 = qkv3.shape
    D = d3 // 3
    hw = 2 * HEAD_DIM
    H = D // hw
    hps = _tile(H, ATTN_HEADS)
    bw = hps * hw
    nb = H // hps
    t = _tile(S, ATTN_TILE)
    assert CHUNK & (CHUNK - 1) == 0 and t % CHUNK == 0
    pairs = [(i, j) for i in range(S // t) for j in range(i + 1)]
    qi = jnp.asarray(np.array([p[0] for p in pairs], np.int32))
    kj = jnp.asarray(np.array([p[1] for p in pairs], np.int32))
    return pl.pallas_call(
        functools.partial(_attn_body, lambda_init=lambda_init),
        grid_spec=pltpu.PrefetchScalarGridSpec(
            num_scalar_prefetch=2,
            grid=(B, nb, len(pairs)),
            in_specs=[pl.BlockSpec((None, 4, HEAD_DIM), lambda b, h, s, qi, kj: (layer, 0, 0)),
                      pl.BlockSpec((None, 1, hw), lambda b, h, s, qi, kj: (layer, 0, 0)),
                      pl.BlockSpec((None, t, bw), lambda b, h, s, qi, kj: (b, qi[s], h)),
                      pl.BlockSpec((None, t, bw), lambda b, h, s, qi, kj: (b, kj[s], nb + h)),
                      pl.BlockSpec((None, t, bw), lambda b, h, s, qi, kj: (b, kj[s], 2 * nb + h))],
            out_specs=pl.BlockSpec((None, t, bw), lambda b, h, s, qi, kj: (b, qi[s], h)),
            scratch_shapes=[pltpu.VMEM((2 * hps, t, LANES), F32),
                            pltpu.VMEM((2 * hps, t, LANES), F32),
                            pltpu.VMEM((2 * hps, t, hw), F32)]),
        out_shape=jax.ShapeDtypeStruct((B, S, D), BF16),
        compiler_params=_params("parallel", "parallel", "arbitrary"),
        name="diff_attention",
    )(qi, kj, lam_vecs, subln_g, qkv3, qkv3, qkv3)


ROUTE_IDX0, ROUTE_IDX1, ROUTE_W0, ROUTE_W1 = 0, 1, 2, 3


def _pack_halves(x):
    n = x.shape[1] // 2
    bits = lax.bitcast_convert_type(x.astype(jnp.bfloat16).astype(F32), jnp.uint32)
    return bits[:, n:] | lax.shift_right_logical(bits[:, :n], jnp.uint32(16))


def _unpack_halves(w):
    lo = lax.bitcast_convert_type(lax.shift_left(w, jnp.uint32(16)), F32)
    hi = lax.bitcast_convert_type(w & jnp.uint32(0xFFFF0000), F32)
    return lo, hi


def _router_body(x_ref, g_ref, wr_ref, h_ref, route_ref, *, n_experts):
    h = _rms(x_ref[...], g_ref[...])
    h_ref[...] = _pack_halves(h)
    logits = jnp.dot(h, wr_ref[...], preferred_element_type=F32,
                     precision=lax.Precision.HIGHEST)
    col = lax.broadcasted_iota(jnp.int32, logits.shape, 1)
    logits = jnp.where(col < n_experts, logits, -jnp.inf)
    v0 = jnp.max(logits, axis=1, keepdims=True)
    i0 = jnp.min(jnp.where(logits == v0, col, LANES), axis=1, keepdims=True)
    rest = jnp.where(col == i0, -jnp.inf, logits)
    v1 = jnp.max(rest, axis=1, keepdims=True)
    i1 = jnp.min(jnp.where(rest == v1, col, LANES), axis=1, keepdims=True)
    e = jnp.exp(v1 - v0)
    w0 = 1.0 / (1.0 + e)
    w1 = e / (1.0 + e)
    route = jnp.where(col == ROUTE_IDX0, i0.astype(F32), 0.0)
    route = jnp.where(col == ROUTE_IDX1, i1.astype(F32), route)
    route = jnp.where(col == ROUTE_W0, w0, route)
    route = jnp.where(col == ROUTE_W1, w1, route)
    route_ref[...] = route


def _norm_router(x, g, w_router_padded, n_experts):
    T, D = x.shape
    tr = _tile(T, ROW_TILE)
    return pl.pallas_call(
        functools.partial(_router_body, n_experts=n_experts),
        grid=(T // tr,),
        in_specs=[pl.BlockSpec((tr, D), lambda i: (i, 0)),
                  pl.BlockSpec((1, D), lambda i: (0, 0)),
                  pl.BlockSpec((D, LANES), lambda i: (0, 0))],
        out_specs=[pl.BlockSpec((tr, D // 2), lambda i: (i, 0)),
                   pl.BlockSpec((tr, LANES), lambda i: (i, 0))],
        out_shape=[jax.ShapeDtypeStruct((T, D // 2), jnp.uint32),
                   jax.ShapeDtypeStruct((T, LANES), F32)],
        compiler_params=_params("parallel"),
        name="norm_router",
    )(x, g.reshape(1, D), w_router_padded)


def _slot_plan(expert_idx, n_experts, tm):
    T = expert_idx.shape[0]
    flat = expert_idx.reshape(T * TOP_K)
    onehot = (flat[:, None] == jnp.arange(n_experts, dtype=jnp.int32)[None, :]).astype(jnp.int32)
    csum = jnp.cumsum(onehot, axis=0)
    pos = jnp.sum(csum * onehot, axis=1) - 1
    counts = csum[-1]
    padded = (counts + tm - 1) // tm * tm
    gend = jnp.cumsum(padded)
    gstart = gend - padded
    slot = (jnp.sum(onehot * gstart[None, :], axis=1) + pos).astype(jnp.int32)
    n_tiles = (T * TOP_K) // tm + n_experts
    tile_start = jnp.arange(n_tiles, dtype=jnp.int32) * tm
    tile_expert = jnp.sum((tile_start[:, None] >= gend[None, :]).astype(jnp.int32), axis=1)
    n_valid = (gend[-1] // tm).astype(jnp.int32).reshape(1)
    tile_expert = jnp.minimum(tile_expert, n_experts - 1).astype(jnp.int32)
    token = jnp.arange(T * TOP_K, dtype=jnp.int32) // TOP_K
    src = jnp.zeros((n_tiles * tm,), jnp.int32).at[slot].set(token)
    return slot, src, tile_expert, n_valid, n_tiles


def _row_copies(index, n_rows, src_ref, dst_ref, sem):
    def copy(r):
        return pltpu.make_async_copy(src_ref.at[pl.ds(index(r), 1), :],
                                     dst_ref.at[pl.ds(r, 1), :], sem)

    def start_one(r, carry):
        copy(r).start()
        return carry

    def wait_one(r, carry):
        copy(r).wait()
        return carry

    start = lambda: lax.fori_loop(0, n_rows, start_one, 0, unroll=8)
    wait = lambda: lax.fori_loop(0, n_rows, wait_one, 0, unroll=8)
    return start, wait


def _gather_slots_body(src_ref, nv_ref, h_ref, o_ref, sem, *, tm):
    i = pl.program_id(0)
    valid = i < nv_ref[0]

    @pl.when(valid)
    def _():
        start, wait = _row_copies(lambda r: src_ref[i * tm + r], tm, h_ref, o_ref, sem)
        start()
        wait()

    @pl.when(jnp.logical_not(valid))
    def _():
        o_ref[...] = jnp.zeros(o_ref.shape, o_ref.dtype)


def _gather_slots(src, n_valid, hp, n_tiles, tm):
    _, W = hp.shape
    return pl.pallas_call(
        functools.partial(_gather_slots_body, tm=tm),
        grid_spec=pltpu.PrefetchScalarGridSpec(
            num_scalar_prefetch=2,
            grid=(n_tiles,),
            in_specs=[pl.BlockSpec(memory_space=pl.ANY)],
            out_specs=pl.BlockSpec((tm, W), lambda i, src, nv: (i, 0)),
            scratch_shapes=[pltpu.SemaphoreType.DMA(())]),
        out_shape=jax.ShapeDtypeStruct((n_tiles * tm, W), hp.dtype),
        compiler_params=_params("arbitrary"),
        name="gather_slots",
    )(src, n_valid, hp)


def _moe_up_body(a_ref, wg_ref, wu_ref, o_ref):
    lo, hi = _unpack_halves(a_ref[...])
    a = jnp.concatenate([lo.astype(BF16), hi.astype(BF16)], axis=1)
    g = jnp.dot(a, wg_ref[...], preferred_element_type=F32)
    u = jnp.dot(a, wu_ref[...], preferred_element_type=F32)
    o_ref[...] = (jax.nn.silu(g) * u).astype(o_ref.dtype)


def _moe_down_body(a_ref, w_ref, o_ref):
    o_ref[...] = _pack_halves(jnp.dot(a_ref[...], w_ref[...], preferred_element_type=F32))


def _grouped_body(te_ref, nv_ref, *refs, body):
    del te_ref
    valid = pl.program_id(1) < nv_ref[0]
    o_ref = refs[-1]

    @pl.when(valid)
    def _():
        body(*refs)

    @pl.when(jnp.logical_not(valid))
    def _():
        o_ref[...] = jnp.zeros(o_ref.shape, o_ref.dtype)


def _grouped_call(body, tile_expert, n_valid, a, weights, layer, n_tiles, tm, name, *, packed_out):
    P, ka = a.shape
    K, N = weights[0].shape[2:]
    tn = _tile(N, MM_TN)
    pack = 2 if packed_out else 1
    row = lambda n, i, te, nv: jnp.maximum(jnp.minimum(i, nv[0] - 1), 0)
    w_spec = pl.BlockSpec((None, None, K, tn),
                          lambda n, i, te, nv: (layer, te[row(n, i, te, nv)], 0, n))
    return pl.pallas_call(
        functools.partial(_grouped_body, body=body),
        grid_spec=pltpu.PrefetchScalarGridSpec(
            num_scalar_prefetch=2,
            grid=(N // tn, n_tiles),
            in_specs=[pl.BlockSpec((tm, ka), lambda n, i, te, nv: (row(n, i, te, nv), 0))]
            + [w_spec] * len(weights),
            out_specs=pl.BlockSpec((tm, tn // pack), lambda n, i, te, nv: (i, n))),
        out_shape=jax.ShapeDtypeStruct((P, N // pack), jnp.uint32 if packed_out else BF16),
        compiler_params=_params("parallel", "arbitrary"),
        name=name,
    )(tile_expert, n_valid, a, *weights)


def _combine_body(slot_ref, x_ref, route_ref, y_ref, g_ref, *rest, emit_x, tn):
    out_refs, (ybuf, sem) = rest[:-2], rest[-2:]
    tr = x_ref.shape[0]
    base = pl.program_id(0) * tr
    loops = [_row_copies(lambda r, k=k: slot_ref[TOP_K * (base + r) + k], tr, y_ref, ybuf.at[k], sem)
             for k in range(TOP_K)]
    for start, _ in loops:
        start()
    for _, wait in loops:
        wait()
    half = tn // 2
    y = None
    for k, col in enumerate((ROUTE_W0, ROUTE_W1)):
        lo, hi = _unpack_halves(ybuf[k])
        parts = []
        for n in range(lo.shape[1] // half):
            parts += [lo[:, n * half:(n + 1) * half], hi[:, n * half:(n + 1) * half]]
        yk = route_ref[:, col:col + 1] * jnp.concatenate(parts, axis=1)
        y = yk if y is None else y + yk
    x = x_ref[...] + y
    if emit_x:
        out_refs[0][...] = x
    h_ref = out_refs[-1]
    h_ref[...] = _rms(x, g_ref[...]).astype(h_ref.dtype)


def _combine_norm(slot, x, route, yp, g, tn, *, emit_x, h_dtype):
    T, D = x.shape
    tr = _tile(T, ROW_TILE)
    row = pl.BlockSpec((tr, D), lambda i, slot: (i, 0))
    outs = ([jax.ShapeDtypeStruct((T, D), F32)] if emit_x else []) + \
        [jax.ShapeDtypeStruct((T, D), h_dtype)]
    return pl.pallas_call(
        functools.partial(_combine_body, emit_x=emit_x, tn=tn),
        grid_spec=pltpu.PrefetchScalarGridSpec(
            num_scalar_prefetch=1,
            grid=(T // tr,),
            in_specs=[row,
                      pl.BlockSpec((tr, LANES), lambda i, slot: (i, 0)),
                      pl.BlockSpec(memory_space=pl.ANY),
                      pl.BlockSpec((1, D), lambda i, slot: (0, 0))],
            out_specs=[row] * len(outs),
            scratch_shapes=[pltpu.VMEM((TOP_K, tr, D // 2), jnp.uint32),
                            pltpu.SemaphoreType.DMA(())]),
        out_shape=outs,
        compiler_params=_params("arbitrary"),
        name="combine_norm",
    )(slot, x, route, yp, g.reshape(1, D))


def _moe(x, norm_g, w_router, w_gate, w_up, w_down, layer, next_g, *, emit_x, h_dtype):
    T, D = x.shape
    n_experts = w_router.shape[2]
    tm = _tile(T * TOP_K // n_experts, MOE_TM)
    wr = jnp.pad(w_router[layer], ((0, 0), (0, LANES - n_experts)))
    hp, route = _norm_router(x, norm_g, wr, n_experts)
    expert_idx = route[:, ROUTE_IDX0:ROUTE_IDX1 + 1].astype(jnp.int32)
    slot, src, tile_expert, n_valid, n_tiles = _slot_plan(expert_idx, n_experts, tm)
    xg = _gather_slots(src, n_valid, hp, n_tiles, tm)
    act = _grouped_call(_moe_up_body, tile_expert, n_valid, xg, [w_gate, w_up], layer,
                        n_tiles, tm, "moe_up", packed_out=False)
    yp = _grouped_call(_moe_down_body, tile_expert, n_valid, act, [w_down], layer,
                       n_tiles, tm, "moe_down", packed_out=True)
    return _combine_norm(slot, x, route, yp, next_g, _tile(D, MM_TN), emit_x=emit_x, h_dtype=h_dtype)


def kernel(x, ev_norm1, ev_w_in, ev_conv_w, ev_conv_b, ev_gate_a_w, ev_gate_a_b, ev_gate_x_w, ev_gate_x_b, ev_lru_lambda, ev_pool_w, ev_pool_scale, ev_w_out, ev_norm2, ev_ffn_gate, ev_ffn_up, ev_ffn_down, od_norm1, od_w_qkv, od_lambda, od_subln, od_w_o, od_norm2, od_router, od_exp_gate, od_exp_up, od_exp_down, final_norm):
    B, S, D = x.shape
    T = B * S
    n_even, n_odd = ev_norm1.shape[0], od_norm1.shape[0]
    depth = n_even + n_odd
    d_lru = ev_lru_lambda.shape[1]
    d_pool = ev_pool_scale.shape[1]

    w_in, w_out = ev_w_in.astype(BF16), ev_w_out.astype(BF16)
    gate_a_w, gate_x_w = ev_gate_a_w.astype(BF16), ev_gate_x_w.astype(BF16)
    pool_w = ev_pool_w.astype(BF16)
    ffn_gate, ffn_up, ffn_down = (w.astype(BF16) for w in (ev_ffn_gate, ev_ffn_up, ev_ffn_down))
    w_qkv, w_o = od_w_qkv.astype(BF16), od_w_o.astype(BF16)
    exp_gate, exp_up, exp_down = (w.astype(BF16) for w in (od_exp_gate, od_exp_up, od_exp_down))
    row3 = lambda v: v.reshape(v.shape[0], 1, v.shape[1])

    xs = x.reshape(T, D)
    h = None
    out = None
    for layer in range(depth):
        j = layer // 2
        if layer % 2 == 0:
            if h is None:
                u = _norm_matmul(xs, ev_norm1, [w_in], j, out_dtype=F32)
            else:
                u = _matmul([(h, w_in, 0)], j, w_in.shape[2], out_dtype=F32)
            u3 = u.reshape(B, S, -1)
            y_lru = _lru(u3, j, d_lru, ev_conv_w, row3(ev_conv_b), gate_a_w, row3(ev_gate_a_b),
                         gate_x_w, row3(ev_gate_x_b), row3(ev_lru_lambda))
            assert (2 * d_lru) % d_pool == 0
            y_pool = _pool(u3, j, 2 * d_lru // d_pool, d_pool, pool_w, row3(ev_pool_scale))
            assert d_lru == d_pool
            xs = _matmul([(y_lru.reshape(T, d_lru), w_out, 0), (y_pool.reshape(T, d_pool), w_out, 1)],
                         j, D, res=xs, out_dtype=F32)
            act = _norm_matmul(xs, ev_norm2, [ffn_gate, ffn_up], j, out_dtype=BF16, swiglu=True)
            xs = _matmul([(act, ffn_down, 0)], j, D, res=xs, out_dtype=F32, split_k=True)
            h = None
        else:
            lambda_init = 0.8 - 0.6 * math.exp(-0.3 * layer)
            q_scale = (HEAD_DIM ** -0.5 * math.log2(math.e), D)
            if h is None:
                qkv = _norm_matmul(xs, od_norm1, [w_qkv], j, out_dtype=BF16, lead_cols_scale=q_scale)
            else:
                qkv = _matmul([(h, w_qkv, 0)], j, 3 * D, out_dtype=BF16, lead_cols_scale=q_scale)
            qkv3 = qkv.reshape(B, S, 3 * D)
            o = _attention(qkv3, j, od_lambda, row3(od_subln), lambda_init)
            xs = _matmul([(o.reshape(T, D), w_o, 0)], j, D, res=xs, out_dtype=F32)
            last = layer == depth - 1
            if last:
                next_g = final_norm
            else:
                next_g = ev_norm1[j + 1]
            res = _moe(xs, od_norm2[j], od_router, exp_gate, exp_up, exp_down, j, next_g,
                       emit_x=not last, h_dtype=F32 if last else BF16)
            if last:
                out = res[0]
            else:
                xs, h = res
    if out is None:
        out = _rmsnorm(xs, final_norm, F32)
    return out.reshape(B, S, D)
```

```python
import functools
import math

import jax
import jax.numpy as jnp
import numpy as np
from jax import lax
from jax.experimental import pallas as pl
from jax.experimental.pallas import tpu as pltpu

F32 = jnp.float32
BF16 = jnp.bfloat16

EPS = 1e-6
NEG_INF = -1e30
CHUNK = 64
LRU_C = 8.0
LRU_BLOCK = 128
POOL_WINDOWS = (2, 4, 8, 16)
HEAD_DIM = 128
TOP_K = 2

LANES = 128
SUBLANES = 8
VMEM_PHYSICAL_BYTES = 64 * 1024 * 1024
VMEM_LIMIT_BYTES = VMEM_PHYSICAL_BYTES - 8 * 1024 * 1024

CONV_HIST = SUBLANES
POOL_HIST = 2 * SUBLANES

ROW_TILE = 256
MM_TM = 1024
MM_TN = 1024
MM_TN_F32 = 512
MM_TK = 2048
SEQ_TILE = 512
LRU_CB = 2048
ATTN_TILE = 512
ATTN_HEADS = 4
MOE_TM = 512
DMA_ROWS = 256


def _tile(dim, pref):
    t = min(dim, pref)
    assert dim % t == 0, (dim, pref)
    return t


def _params(*semantics):
    return pltpu.CompilerParams(dimension_semantics=semantics,
                                vmem_limit_bytes=VMEM_LIMIT_BYTES)


def _rms(x, g):
    return x * lax.rsqrt(jnp.mean(x * x, axis=-1, keepdims=True) + EPS) * g


def _rmsnorm_body(x_ref, g_ref, o_ref):
    o_ref[...] = _rms(x_ref[...], g_ref[...]).astype(o_ref.dtype)


def _rmsnorm(x, g, out_dtype):
    T, D = x.shape
    tr = _tile(T, ROW_TILE)
    return pl.pallas_call(
        _rmsnorm_body,
        grid=(T // tr,),
        in_specs=[pl.BlockSpec((tr, D), lambda i: (i, 0)),
                  pl.BlockSpec((1, D), lambda i: (0, 0))],
        out_specs=pl.BlockSpec((tr, D), lambda i: (i, 0)),
        out_shape=jax.ShapeDtypeStruct((T, D), out_dtype),
        compiler_params=_params("parallel"),
        name="rmsnorm",
    )(x, g.reshape(1, D))


def _mm_body(*refs, n_pairs, has_res, nk, lead_scale):
    pairs = [(refs[2 * p], refs[2 * p + 1]) for p in range(n_pairs)]
    pos = 2 * n_pairs
    res_ref = refs[pos] if has_res else None
    o_ref = refs[pos + has_res]
    part = None
    for a_ref, w_ref in pairs:
        d = jnp.dot(a_ref[...], w_ref[...], preferred_element_type=F32)
        part = d if part is None else part + d

    def finish(total):
        if lead_scale is not None:
            scale, n_tiles = lead_scale
            total = total * jnp.where(pl.program_id(1) < n_tiles, scale, 1.0).astype(F32)
        if has_res:
            total = res_ref[...] + total
        o_ref[...] = total.astype(o_ref.dtype)

    if nk == 1:
        finish(part)
    else:
        acc_ref = refs[pos + has_res + 1]
        k = pl.program_id(2)

        @pl.when(k == 0)
        def _():
            acc_ref[...] = part

        @pl.when(k > 0)
        def _():
            acc_ref[...] += part

        @pl.when(k == nk - 1)
        def _():
            finish(acc_ref[...])


def _matmul(pairs, layer, n_out, *, res=None, out_dtype, split_k=False, lead_cols_scale=None):
    M = pairs[0][0].shape[0]
    tm = _tile(M, MM_TM)
    tn = _tile(n_out, MM_TN if res is None else MM_TN_F32)
    lead_scale = None
    if lead_cols_scale is not None:
        assert not split_k and lead_cols_scale[1] % tn == 0
        lead_scale = (lead_cols_scale[0], lead_cols_scale[1] // tn)
    if split_k:
        assert len(pairs) == 1
        tk = _tile(pairs[0][0].shape[1], MM_TK)
        tn = _tile(n_out, MM_TN)
        nk = pairs[0][0].shape[1] // tk
    else:
        nk = 1
    in_specs, args = [], []
    for a, w, rb in pairs:
        ka = a.shape[1]
        if nk == 1:
            in_specs.append(pl.BlockSpec((tm, ka), lambda i, j, k: (i, 0)))
            in_specs.append(pl.BlockSpec((None, ka, tn),
                                         lambda i, j, k, rb=rb: (layer, rb, j)))
        else:
            in_specs.append(pl.BlockSpec((tm, tk), lambda i, j, k: (i, k)))
            in_specs.append(pl.BlockSpec((None, tk, tn), lambda i, j, k: (layer, k, j)))
        args += [a, w]
    if res is not None:
        in_specs.append(pl.BlockSpec((tm, tn), lambda i, j, k: (i, j)))
        args.append(res)
    scratch = [pltpu.VMEM((tm, tn), F32)] if nk > 1 else []
    return pl.pallas_call(
        functools.partial(_mm_body, n_pairs=len(pairs), has_res=res is not None, nk=nk,
                          lead_scale=lead_scale),
        grid=(M // tm, n_out // tn, nk),
        in_specs=in_specs,
        out_specs=pl.BlockSpec((tm, tn), lambda i, j, k: (i, j)),
        out_shape=jax.ShapeDtypeStruct((M, n_out), out_dtype),
        scratch_shapes=scratch,
        compiler_params=_params("parallel", "parallel", "arbitrary"),
        name="matmul",
    )(*args)


def _swiglu_body(a_ref, wg_ref, wu_ref, o_ref):
    a = a_ref[...]
    g = jnp.dot(a, wg_ref[...], preferred_element_type=F32)
    u = jnp.dot(a, wu_ref[...], preferred_element_type=F32)
    o_ref[...] = (jax.nn.silu(g) * u).astype(o_ref.dtype)


def _swiglu_up(a, wg, wu, layer):
    M, K = a.shape
    N = wg.shape[2]
    tm = _tile(M, MM_TM)
    tn = _tile(N, MM_TN_F32)
    return pl.pallas_call(
        _swiglu_body,
        grid=(M // tm, N // tn),
        in_specs=[pl.BlockSpec((tm, K), lambda i, j: (i, 0)),
                  pl.BlockSpec((None, K, tn), lambda i, j: (layer, 0, j)),
                  pl.BlockSpec((None, K, tn), lambda i, j: (layer, 0, j))],
        out_specs=pl.BlockSpec((tm, tn), lambda i, j: (i, j)),
        out_shape=jax.ShapeDtypeStruct((M, N), BF16),
        compiler_params=_params("parallel", "parallel"),
        name="swiglu_up",
    )(a, wg, wu)


def _lru_body(x_ref, g_ref, cw_ref, cb_ref, wa_ref, ba_ref, wx_ref, bx_ref, lam_ref, o_ref,
              xpad_ref, a_scr, u_scr, h_scr, hstate_ref, *, ts, nblk):
    s = pl.program_id(2)

    @pl.when(s == 0)
    def _():
        xpad_ref[0:CONV_HIST, :] = jnp.zeros((CONV_HIST, xpad_ref.shape[1]), F32)
        hstate_ref[...] = jnp.zeros(hstate_ref.shape, F32)

    xpad_ref[CONV_HIST:CONV_HIST + ts, :] = x_ref[...]
    kw = cw_ref.shape[0]
    xc = cw_ref[kw - 1:kw, :] * x_ref[...]
    for k in range(kw - 1):
        off = CONV_HIST - (kw - 1) + k
        xc = xc + cw_ref[k:k + 1, :] * xpad_ref[off:off + ts, :]
    xc = xc + cb_ref[...]
    xpad_ref[0:CONV_HIST, :] = x_ref[ts - CONV_HIST:ts, :]

    lam = lam_ref[...]
    sp = jnp.maximum(-lam, 0.0) + jnp.log1p(jnp.exp(-jnp.abs(lam)))
    xcb = xc.astype(BF16)
    for n in range(nblk):
        sl = slice(n * LRU_BLOCK, (n + 1) * LRU_BLOCK)
        ga = jnp.dot(xcb[:, sl], wa_ref[n], preferred_element_type=F32) + ba_ref[:, sl]
        gx = jnp.dot(xcb[:, sl], wx_ref[n], preferred_element_type=F32) + bx_ref[:, sl]
        log_a = -LRU_C * jax.nn.sigmoid(ga) * sp[:, sl]
        th = jnp.tanh(log_a)
        mult = jnp.sqrt(-2.0 * th / (1.0 - th))
        a_scr[:, sl] = jnp.exp(log_a)
        u_scr[:, sl] = mult * jax.nn.sigmoid(gx) * xc[:, sl]

    def step(t, h):
        h = a_scr[pl.ds(t, 1), :] * h + u_scr[pl.ds(t, 1), :]
        h_scr[pl.ds(t, 1), :] = h
        return h

    hstate_ref[...] = lax.fori_loop(0, ts, step, hstate_ref[...], unroll=8)
    o_ref[...] = (h_scr[...] * jax.nn.gelu(g_ref[...])).astype(o_ref.dtype)


def _lru(u3, layer, d_lru, conv_w, conv_b, wa, ba, wx, bx, lam):
    B, S, _ = u3.shape
    ts = _tile(S, SEQ_TILE)
    cb = _tile(d_lru, LRU_CB)
    nblk = cb // LRU_BLOCK
    kw = conv_w.shape[1]
    gate_col0 = d_lru // cb
    vec = lambda: pl.BlockSpec((None, 1, cb), lambda b, c, s: (layer, 0, c))
    gate_w = lambda: pl.BlockSpec((None, nblk, LRU_BLOCK, LRU_BLOCK),
                                  lambda b, c, s: (layer, c, 0, 0))
    return pl.pallas_call(
        functools.partial(_lru_body, ts=ts, nblk=nblk),
        grid=(B, d_lru // cb, S // ts),
        in_specs=[pl.BlockSpec((None, ts, cb), lambda b, c, s: (b, s, c)),
                  pl.BlockSpec((None, ts, cb), lambda b, c, s: (b, s, gate_col0 + c)),
                  pl.BlockSpec((None, kw, cb), lambda b, c, s: (layer, 0, c)),
                  vec(), gate_w(), vec(), gate_w(), vec(), vec()],
        out_specs=pl.BlockSpec((None, ts, cb), lambda b, c, s: (b, s, c)),
        out_shape=jax.ShapeDtypeStruct((B, S, d_lru), BF16),
        scratch_shapes=[pltpu.VMEM((CONV_HIST + ts, cb), F32),
                        pltpu.VMEM((ts, cb), F32),
                        pltpu.VMEM((ts, cb), F32),
                        pltpu.VMEM((ts, cb), F32),
                        pltpu.VMEM((1, cb), F32)],
        compiler_params=_params("parallel", "parallel", "arbitrary"),
        name="lru",
    )(u3, u3, conv_w, conv_b, wa, ba, wx, bx, lam)


def _pool_body(x_ref, w_ref, sc_ref, o_ref, xpad_ref, *, ts, gw):
    s = pl.program_id(1)

    @pl.when(s == 0)
    def _():
        xpad_ref[0:POOL_HIST, :] = jnp.zeros((POOL_HIST, xpad_ref.shape[1]), F32)

    xpad_ref[POOL_HIST:POOL_HIST + ts, :] = x_ref[...]
    frames = (s * ts + 1 + lax.broadcasted_iota(jnp.int32, (ts, 1), 0)).astype(F32)
    for g, win in enumerate(POOL_WINDOWS):
        cs = slice(g * gw, (g + 1) * gw)
        acc = x_ref[:, cs]
        for k in range(1, win):
            acc = acc + xpad_ref[POOL_HIST - k:POOL_HIST - k + ts, cs]
        y = acc / jnp.minimum(frames, float(win)) - x_ref[:, cs]
        z = jnp.dot(y.astype(BF16), w_ref[g], preferred_element_type=F32) * sc_ref[:, cs]
        o_ref[:, cs] = z.astype(o_ref.dtype)
    xpad_ref[0:POOL_HIST, :] = x_ref[ts - POOL_HIST:ts, :]


def _pool(u3, layer, col_block, d_pool, w_pool, scale):
    B, S, _ = u3.shape
    ts = _tile(S, SEQ_TILE)
    ng = len(POOL_WINDOWS)
    gw = d_pool // ng
    assert max(POOL_WINDOWS) <= POOL_HIST <= ts
    return pl.pallas_call(
        functools.partial(_pool_body, ts=ts, gw=gw),
        grid=(B, S // ts),
        in_specs=[pl.BlockSpec((None, ts, d_pool), lambda b, s: (b, s, col_block)),
                  pl.BlockSpec((None, ng, gw, gw), lambda b, s: (layer, 0, 0, 0)),
                  pl.BlockSpec((None, 1, d_pool), lambda b, s: (layer, 0, 0))],
        out_specs=pl.BlockSpec((None, ts, d_pool), lambda b, s: (b, s, 0)),
        out_shape=jax.ShapeDtypeStruct((B, S, d_pool), BF16),
        scratch_shapes=[pltpu.VMEM((POOL_HIST + ts, d_pool), F32)],
        compiler_params=_params("parallel", "arbitrary"),
        name="pool",
    )(u3, w_pool, scale)


def _lanes(x, width):
    return jnp.concatenate([x] * (width // LANES), axis=1) if width > LANES else x


def _attn_tile(q_ref, k_ref, v_ref, m_scr, l_scr, acc_scr, *, masked):
    hw = acc_scr.shape[2]
    for r in range(m_scr.shape[0]):
        head, c = divmod(r, 2)
        cs = slice(head * hw + c * HEAD_DIM, head * hw + (c + 1) * HEAD_DIM)
        s = lax.dot_general(q_ref[:, cs], k_ref[:, cs], (((1,), (1,)), ((), ())),
                            preferred_element_type=F32)
        if masked:
            shift = CHUNK.bit_length() - 1
            qc = lax.shift_right_logical(lax.broadcasted_iota(jnp.int32, s.shape, 0), shift)
            kc = lax.shift_right_logical(lax.broadcasted_iota(jnp.int32, s.shape, 1), shift)
            s = jnp.where(kc <= qc, s, NEG_INF)
        m_prev = m_scr[r]
        m_new = jnp.maximum(m_prev, jnp.max(s, axis=1, keepdims=True))
        alpha = jnp.exp2(m_prev - m_new)
        p = jnp.exp2(s - _lanes(m_new, s.shape[1]))
        l_scr[r] = alpha * l_scr[r] + jnp.sum(p, axis=1, keepdims=True)
        acc_scr[r] = (_lanes(alpha, hw) * acc_scr[r]
                      + jnp.dot(p.astype(BF16), v_ref[:, head * hw:(head + 1) * hw],
                                preferred_element_type=F32))
        m_scr[r] = m_new


def _attn_body(qi_ref, kj_ref, lam_ref, g_ref, q_ref, k_ref, v_ref, o_ref, m_scr, l_scr, acc_scr,
               *, lambda_init):
    step = pl.program_id(2)
    i = qi_ref[step]
    j = kj_ref[step]

    @pl.when(j == 0)
    def _():
        m_scr[...] = jnp.full(m_scr.shape, NEG_INF, F32)
        l_scr[...] = jnp.zeros(l_scr.shape, F32)
        acc_scr[...] = jnp.zeros(acc_scr.shape, F32)

    tile = functools.partial(_attn_tile, q_ref, k_ref, v_ref, m_scr, l_scr, acc_scr)

    @pl.when(j < i)
    def _():
        tile(masked=False)

    @pl.when(j == i)
    def _():
        tile(masked=True)
        lv = lam_ref[...]
        lam = (jnp.exp(jnp.sum(lv[0:1] * lv[1:2], axis=1, keepdims=True))
               - jnp.exp(jnp.sum(lv[2:3] * lv[3:4], axis=1, keepdims=True)) + lambda_init)
        hw = acc_scr.shape[2]
        for head in range(m_scr.shape[0] // 2):
            r = 2 * head
            o = (acc_scr[r] / _lanes(l_scr[r], hw)
                 - lam * (acc_scr[r + 1] / _lanes(l_scr[r + 1], hw)))
            o_ref[:, head * hw:(head + 1) * hw] = (
                _rms(o, g_ref[...]) * (1.0 - lambda_init)).astype(o_ref.dtype)


def _attention(qkv3, layer, lam_vecs, subln_g, lambda_init):
    B, S, d3 = qkv3.shape
    D = d3 // 3
    hw = 2 * HEAD_DIM
    H = D // hw
    hps = _tile(H, ATTN_HEADS)
    bw = hps * hw
    nb = H // hps
    t = _tile(S, ATTN_TILE)
    assert CHUNK & (CHUNK - 1) == 0 and t % CHUNK == 0
    pairs = [(i, j) for i in range(S // t) for j in range(i + 1)]
    qi = jnp.asarray(np.array([p[0] for p in pairs], np.int32))
    kj = jnp.asarray(np.array([p[1] for p in pairs], np.int32))
    return pl.pallas_call(
        functools.partial(_attn_body, lambda_init=lambda_init),
        grid_spec=pltpu.PrefetchScalarGridSpec(
            num_scalar_prefetch=2,
            grid=(B, nb, len(pairs)),
            in_specs=[pl.BlockSpec((None, 4, HEAD_DIM), lambda b, h, s, qi, kj: (layer, 0, 0)),
                      pl.BlockSpec((None, 1, hw), lambda b, h, s, qi, kj: (layer, 0, 0)),
                      pl.BlockSpec((None, t, bw), lambda b, h, s, qi, kj: (b, qi[s], h)),
                      pl.BlockSpec((None, t, bw), lambda b, h, s, qi, kj: (b, kj[s], nb + h)),
                      pl.BlockSpec((None, t, bw), lambda b, h, s, qi, kj: (b, kj[s], 2 * nb + h))],
            out_specs=pl.BlockSpec((None, t, bw), lambda b, h, s, qi, kj: (b, qi[s], h)),
            scratch_shapes=[pltpu.VMEM((2 * hps, t, LANES), F32),
                            pltpu.VMEM((2 * hps, t, LANES), F32),
                            pltpu.VMEM((2 * hps, t, hw), F32)]),
        out_shape=jax.ShapeDtypeStruct((B, S, D), BF16),
        compiler_params=_params("parallel", "parallel", "arbitrary"),
        name="diff_attention",
    )(qi, kj, lam_vecs, subln_g, qkv3, qkv3, qkv3)


ROUTE_IDX0, ROUTE_IDX1, ROUTE_W0, ROUTE_W1 = 0, 1, 2, 3


def _pack_halves(x):
    n = x.shape[1] // 2
    bits = lax.bitcast_convert_type(x.astype(jnp.bfloat16).astype(F32), jnp.uint32)
    return bits[:, n:] | lax.shift_right_logical(bits[:, :n], jnp.uint32(16))


def _unpack_halves(w):
    lo = lax.bitcast_convert_type(lax.shift_left(w, jnp.uint32(16)), F32)
    hi = lax.bitcast_convert_type(w & jnp.uint32(0xFFFF0000), F32)
    return lo, hi


def _router_body(x_ref, g_ref, whi_ref, wlo_ref, h_ref, route_ref, *, n_experts):
    h = _rms(x_ref[...], g_ref[...])
    h_ref[...] = _pack_halves(h)
    h_hi = h.astype(BF16)
    h_lo = (h - h_hi.astype(F32)).astype(BF16)
    logits = (jnp.dot(h_hi, whi_ref[...], preferred_element_type=F32)
              + (jnp.dot(h_hi, wlo_ref[...], preferred_element_type=F32)
                 + jnp.dot(h_lo, whi_ref[...], preferred_element_type=F32)))
    col = lax.broadcasted_iota(jnp.int32, logits.shape, 1)
    logits = jnp.where(col < n_experts, logits, -jnp.inf)
    v0 = jnp.max(logits, axis=1, keepdims=True)
    i0 = jnp.min(jnp.where(logits == v0, col, LANES), axis=1, keepdims=True)
    rest = jnp.where(col == i0, -jnp.inf, logits)
    v1 = jnp.max(rest, axis=1, keepdims=True)
    i1 = jnp.min(jnp.where(rest == v1, col, LANES), axis=1, keepdims=True)
    e = jnp.exp(v1 - v0)
    w0 = 1.0 / (1.0 + e)
    w1 = e / (1.0 + e)
    route = jnp.where(col == ROUTE_IDX0, i0.astype(F32), 0.0)
    route = jnp.where(col == ROUTE_IDX1, i1.astype(F32), route)
    route = jnp.where(col == ROUTE_W0, w0, route)
    route = jnp.where(col == ROUTE_W1, w1, route)
    route_ref[...] = route


def _norm_router(x, g, w_router, n_experts):
    T, D = x.shape
    tr = _tile(T, ROW_TILE)
    wr = jnp.pad(w_router, ((0, 0), (0, LANES - n_experts)))
    w_hi = wr.astype(BF16)
    w_lo = (wr - w_hi.astype(F32)).astype(BF16)
    return pl.pallas_call(
        functools.partial(_router_body, n_experts=n_experts),
        grid=(T // tr,),
        in_specs=[pl.BlockSpec((tr, D), lambda i: (i, 0)),
                  pl.BlockSpec((1, D), lambda i: (0, 0)),
                  pl.BlockSpec((D, LANES), lambda i: (0, 0)),
                  pl.BlockSpec((D, LANES), lambda i: (0, 0))],
        out_specs=[pl.BlockSpec((tr, D // 2), lambda i: (i, 0)),
                   pl.BlockSpec((tr, LANES), lambda i: (i, 0))],
        out_shape=[jax.ShapeDtypeStruct((T, D // 2), jnp.uint32),
                   jax.ShapeDtypeStruct((T, LANES), F32)],
        compiler_params=_params("parallel"),
        name="norm_router",
    )(x, g.reshape(1, D), w_hi, w_lo)


def _slot_plan(expert_idx, n_experts, tm):
    T = expert_idx.shape[0]
    flat = expert_idx.reshape(T * TOP_K)
    onehot = (flat[:, None] == jnp.arange(n_experts, dtype=jnp.int32)[None, :]).astype(jnp.int32)
    csum = jnp.cumsum(onehot, axis=0)
    pos = jnp.sum(csum * onehot, axis=1) - 1
    counts = csum[-1]
    padded = (counts + tm - 1) // tm * tm
    gend = jnp.cumsum(padded)
    gstart = gend - padded
    slot = (jnp.sum(onehot * gstart[None, :], axis=1) + pos).astype(jnp.int32)
    n_tiles = (T * TOP_K) // tm + n_experts
    tile_start = jnp.arange(n_tiles, dtype=jnp.int32) * tm
    tile_expert = jnp.sum((tile_start[:, None] >= gend[None, :]).astype(jnp.int32), axis=1)
    n_valid = (gend[-1] // tm).astype(jnp.int32).reshape(1)
    tile_expert = jnp.minimum(tile_expert, n_experts - 1).astype(jnp.int32)
    token = jnp.arange(T * TOP_K, dtype=jnp.int32) // TOP_K
    src = jnp.zeros((n_tiles * tm,), jnp.int32).at[slot].set(token)
    return slot, src, tile_expert, n_valid, n_tiles


def _row_copies(index, n_rows, src_ref, dst_ref, sem):
    def copy(r):
        return pltpu.make_async_copy(src_ref.at[pl.ds(index(r), 1), :],
                                     dst_ref.at[pl.ds(r, 1), :], sem)

    def start_one(r, carry):
        copy(r).start()
        return carry

    def wait_one(r, carry):
        copy(r).wait()
        return carry

    start = lambda: lax.fori_loop(0, n_rows, start_one, 0, unroll=8)
    wait = lambda: lax.fori_loop(0, n_rows, wait_one, 0, unroll=8)
    return start, wait


def _gather_slots_body(src_ref, nv_ref, h_ref, o_ref, sem, *, tm):
    i = pl.program_id(0)
    valid = i < nv_ref[0]

    @pl.when(valid)
    def _():
        start, wait = _row_copies(lambda r: src_ref[i * tm + r], tm, h_ref, o_ref, sem)
        start()
        wait()

    @pl.when(jnp.logical_not(valid))
    def _():
        o_ref[...] = jnp.zeros(o_ref.shape, o_ref.dtype)


def _gather_slots(src, n_valid, hp, n_tiles, tm):
    _, W = hp.shape
    return pl.pallas_call(
        functools.partial(_gather_slots_body, tm=tm),
        grid_spec=pltpu.PrefetchScalarGridSpec(
            num_scalar_prefetch=2,
            grid=(n_tiles,),
            in_specs=[pl.BlockSpec(memory_space=pl.ANY)],
            out_specs=pl.BlockSpec((tm, W), lambda i, src, nv: (i, 0)),
            scratch_shapes=[pltpu.SemaphoreType.DMA(())]),
        out_shape=jax.ShapeDtypeStruct((n_tiles * tm, W), hp.dtype),
        compiler_params=_params("arbitrary"),
        name="gather_slots",
    )(src, n_valid, hp)


def _moe_up_body(a_ref, wg_ref, wu_ref, o_ref):
    lo, hi = _unpack_halves(a_ref[...])
    a = jnp.concatenate([lo.astype(BF16), hi.astype(BF16)], axis=1)
    g = jnp.dot(a, wg_ref[...], preferred_element_type=F32)
    u = jnp.dot(a, wu_ref[...], preferred_element_type=F32)
    o_ref[...] = (jax.nn.silu(g) * u).astype(o_ref.dtype)


def _moe_down_body(a_ref, w_ref, o_ref):
    o_ref[...] = _pack_halves(jnp.dot(a_ref[...], w_ref[...], preferred_element_type=F32))


def _grouped_body(te_ref, nv_ref, *refs, body, n_cast):
    i = pl.program_id(1)
    valid = i < nv_ref[0]
    if n_cast:
        refs, w_scr = refs[:-n_cast], refs[-n_cast:]
        a_ref, w_refs, o_ref = refs[0], refs[1:-1], refs[-1]
        fresh = jnp.logical_or(i == 0, te_ref[i] != te_ref[jnp.maximum(i - 1, 0)])

        @pl.when(jnp.logical_and(valid, fresh))
        def _():
            for w_ref, scr in zip(w_refs, w_scr):
                scr[...] = w_ref[...].astype(scr.dtype)

        refs = (a_ref, *w_scr, o_ref)
    o_ref = refs[-1]

    @pl.when(valid)
    def _():
        body(*refs)

    @pl.when(jnp.logical_not(valid))
    def _():
        o_ref[...] = jnp.zeros(o_ref.shape, o_ref.dtype)


def _grouped_call(body, tile_expert, n_valid, a, weights, layer, n_tiles, tm, name, *, packed_out):
    P, ka = a.shape
    K, N = weights[0].shape[2:]
    tn = _tile(N, MM_TN)
    pack = 2 if packed_out else 1
    n_cast = len(weights) if weights[0].dtype == F32 else 0
    row = lambda n, i, te, nv: jnp.maximum(jnp.minimum(i, nv[0] - 1), 0)
    w_spec = pl.BlockSpec((None, None, K, tn),
                          lambda n, i, te, nv: (layer, te[row(n, i, te, nv)], 0, n))
    return pl.pallas_call(
        functools.partial(_grouped_body, body=body, n_cast=n_cast),
        grid_spec=pltpu.PrefetchScalarGridSpec(
            num_scalar_prefetch=2,
            grid=(N // tn, n_tiles),
            in_specs=[pl.BlockSpec((tm, ka), lambda n, i, te, nv: (row(n, i, te, nv), 0))]
            + [w_spec] * len(weights),
            out_specs=pl.BlockSpec((tm, tn // pack), lambda n, i, te, nv: (i, n)),
            scratch_shapes=[pltpu.VMEM((K, tn), BF16)] * n_cast),
        out_shape=jax.ShapeDtypeStruct((P, N // pack), jnp.uint32 if packed_out else BF16),
        compiler_params=_params("parallel", "arbitrary"),
        name=name,
    )(tile_expert, n_valid, a, *weights)


def _combine_body(slot_ref, x_ref, route_ref, y_ref, g_ref, *rest, emit_x, tn):
    out_refs, (ybuf, sem) = rest[:-2], rest[-2:]
    tr = x_ref.shape[0]
    base = pl.program_id(0) * tr
    loops = [_row_copies(lambda r, k=k: slot_ref[TOP_K * (base + r) + k], tr, y_ref, ybuf.at[k], sem)
             for k in range(TOP_K)]
    for start, _ in loops:
        start()
    for _, wait in loops:
        wait()
    half = tn // 2
    y = None
    for k, col in enumerate((ROUTE_W0, ROUTE_W1)):
        lo, hi = _unpack_halves(ybuf[k])
        parts = []
        for n in range(lo.shape[1] // half):
            parts += [lo[:, n * half:(n + 1) * half], hi[:, n * half:(n + 1) * half]]
        yk = route_ref[:, col:col + 1] * jnp.concatenate(parts, axis=1)
        y = yk if y is None else y + yk
    x = x_ref[...] + y
    if emit_x:
        out_refs[0][...] = x
    h_ref = out_refs[-1]
    h_ref[...] = _rms(x, g_ref[...]).astype(h_ref.dtype)


def _combine_norm(slot, x, route, yp, g, tn, *, emit_x, h_dtype):
    T, D = x.shape
    tr = _tile(T, ROW_TILE)
    row = pl.BlockSpec((tr, D), lambda i, slot: (i, 0))
    outs = ([jax.ShapeDtypeStruct((T, D), F32)] if emit_x else []) + \
        [jax.ShapeDtypeStruct((T, D), h_dtype)]
    return pl.pallas_call(
        functools.partial(_combine_body, emit_x=emit_x, tn=tn),
        grid_spec=pltpu.PrefetchScalarGridSpec(
            num_scalar_prefetch=1,
            grid=(T // tr,),
            in_specs=[row,
                      pl.BlockSpec((tr, LANES), lambda i, slot: (i, 0)),
                      pl.BlockSpec(memory_space=pl.ANY),
                      pl.BlockSpec((1, D), lambda i, slot: (0, 0))],
            out_specs=[row] * len(outs),
            scratch_shapes=[pltpu.VMEM((TOP_K, tr, D // 2), jnp.uint32),
                            pltpu.SemaphoreType.DMA(())]),
        out_shape=outs,
        compiler_params=_params("arbitrary"),
        name="combine_norm",
    )(slot, x, route, yp, g.reshape(1, D))


def _moe(x, norm_g, w_router, w_gate, w_up, w_down, layer, next_g, *, emit_x, h_dtype):
    T, D = x.shape
    n_experts = w_router.shape[2]
    tm = _tile(T * TOP_K // n_experts, MOE_TM)
    hp, route = _norm_router(x, norm_g, w_router[layer], n_experts)
    expert_idx = route[:, ROUTE_IDX0:ROUTE_IDX1 + 1].astype(jnp.int32)
    slot, src, tile_expert, n_valid, n_tiles = _slot_plan(expert_idx, n_experts, tm)
    xg = _gather_slots(src, n_valid, hp, n_tiles, tm)
    act = _grouped_call(_moe_up_body, tile_expert, n_valid, xg, [w_gate, w_up], layer,
                        n_tiles, tm, "moe_up", packed_out=False)
    yp = _grouped_call(_moe_down_body, tile_expert, n_valid, act, [w_down], layer,
                       n_tiles, tm, "moe_down", packed_out=True)
    return _combine_norm(slot, x, route, yp, next_g, _tile(D, MM_TN), emit_x=emit_x, h_dtype=h_dtype)


def kernel(x, ev_norm1, ev_w_in, ev_conv_w, ev_conv_b, ev_gate_a_w, ev_gate_a_b, ev_gate_x_w, ev_gate_x_b, ev_lru_lambda, ev_pool_w, ev_pool_scale, ev_w_out, ev_norm2, ev_ffn_gate, ev_ffn_up, ev_ffn_down, od_norm1, od_w_qkv, od_lambda, od_subln, od_w_o, od_norm2, od_router, od_exp_gate, od_exp_up, od_exp_down, final_norm):
    B, S, D = x.shape
    T = B * S
    n_even, n_odd = ev_norm1.shape[0], od_norm1.shape[0]
    depth = n_even + n_odd
    d_lru = ev_lru_lambda.shape[1]
    d_pool = ev_pool_scale.shape[1]

    w_in, w_out = ev_w_in.astype(BF16), ev_w_out.astype(BF16)
    gate_a_w, gate_x_w = ev_gate_a_w.astype(BF16), ev_gate_x_w.astype(BF16)
    pool_w = ev_pool_w.astype(BF16)
    ffn_gate, ffn_up, ffn_down = (w.astype(BF16) for w in (ev_ffn_gate, ev_ffn_up, ev_ffn_down))
    w_qkv, w_o = od_w_qkv.astype(BF16), od_w_o.astype(BF16)
    exp_gate, exp_up = od_exp_gate.astype(BF16), od_exp_up.astype(BF16)
    exp_down = od_exp_down
    row3 = lambda v: v.reshape(v.shape[0], 1, v.shape[1])

    xs = x.reshape(T, D)
    h = None
    out = None
    for layer in range(depth):
        j = layer // 2
        if layer % 2 == 0:
            if h is None:
                h = _rmsnorm(xs, ev_norm1[j], BF16)
            u3 = _matmul([(h, w_in, 0)], j, w_in.shape[2], out_dtype=F32).reshape(B, S, -1)
            y_lru = _lru(u3, j, d_lru, ev_conv_w, row3(ev_conv_b), gate_a_w, row3(ev_gate_a_b),
                         gate_x_w, row3(ev_gate_x_b), row3(ev_lru_lambda))
            assert (2 * d_lru) % d_pool == 0
            y_pool = _pool(u3, j, 2 * d_lru // d_pool, d_pool, pool_w, row3(ev_pool_scale))
            assert d_lru == d_pool
            xs = _matmul([(y_lru.reshape(T, d_lru), w_out, 0), (y_pool.reshape(T, d_pool), w_out, 1)],
                         j, D, res=xs, out_dtype=F32)
            act = _swiglu_up(_rmsnorm(xs, ev_norm2[j], BF16), ffn_gate, ffn_up, j)
            xs = _matmul([(act, ffn_down, 0)], j, D, res=xs, out_dtype=F32, split_k=True)
            h = None
        else:
            lambda_init = 0.8 - 0.6 * math.exp(-0.3 * layer)
            q_scale = (HEAD_DIM ** -0.5 * math.log2(math.e), D)
            if h is None:
                h = _rmsnorm(xs, od_norm1[j], BF16)
            qkv3 = _matmul([(h, w_qkv, 0)], j, 3 * D, out_dtype=BF16,
                           lead_cols_scale=q_scale).reshape(B, S, 3 * D)
            o = _attention(qkv3, j, od_lambda, row3(od_subln), lambda_init)
            xs = _matmul([(o.reshape(T, D), w_o, 0)], j, D, res=xs, out_dtype=F32)
            last = layer == depth - 1
            if last:
                next_g = final_norm
            else:
                next_g = ev_norm1[j + 1]
            res = _moe(xs, od_norm2[j], od_router, exp_gate, exp_up, exp_down, j, next_g,
                       emit_x=not last, h_dtype=F32 if last else BF16)
            if last:
                out = res[0]
            else:
                xs, h = res
    if out is None:
        out = _rmsnorm(xs, final_norm, F32)
    return out.reshape(B, S, D)
```

```python
import functools
import math

import jax
import jax.numpy as jnp
import numpy as np
from jax import lax
from jax.experimental import pallas as pl
from jax.experimental.pallas import tpu as pltpu

F32 = jnp.float32
BF16 = jnp.bfloat16

EPS = 1e-6
NEG_INF = -1e30
CHUNK = 64
LRU_C = 8.0
LRU_BLOCK = 128
POOL_WINDOWS = (2, 4, 8, 16)
HEAD_DIM = 128
TOP_K = 2

LANES = 128
SUBLANES = 8
VMEM_PHYSICAL_BYTES = 64 * 1024 * 1024
VMEM_LIMIT_BYTES = VMEM_PHYSICAL_BYTES - 8 * 1024 * 1024

CONV_HIST = SUBLANES
POOL_HIST = 2 * SUBLANES

ROW_TILE = 256
MM_TM = 1024
MM_TN = 1024
MM_TN_F32 = 512
MM_TK = 2048
SEQ_TILE = 512
LRU_CB = 2048
ATTN_TILE = 512
ATTN_HEADS = 4
MOE_TM = 512
DMA_ROWS = 256


def _tile(dim, pref):
    t = min(dim, pref)
    assert dim % t == 0, (dim, pref)
    return t


def _params(*semantics):
    return pltpu.CompilerParams(dimension_semantics=semantics,
                                vmem_limit_bytes=VMEM_LIMIT_BYTES)


def _rms(x, g):
    return x * lax.rsqrt(jnp.mean(x * x, axis=-1, keepdims=True) + EPS) * g


def _rmsnorm_body(x_ref, g_ref, o_ref):
    o_ref[...] = _rms(x_ref[...], g_ref[...]).astype(o_ref.dtype)


def _rmsnorm(x, g, out_dtype):
    T, D = x.shape
    tr = _tile(T, ROW_TILE)
    return pl.pallas_call(
        _rmsnorm_body,
        grid=(T // tr,),
        in_specs=[pl.BlockSpec((tr, D), lambda i: (i, 0)),
                  pl.BlockSpec((1, D), lambda i: (0, 0))],
        out_specs=pl.BlockSpec((tr, D), lambda i: (i, 0)),
        out_shape=jax.ShapeDtypeStruct((T, D), out_dtype),
        compiler_params=_params("parallel"),
        name="rmsnorm",
    )(x, g.reshape(1, D))


def _row_rsqrt(ss_ref, d):
    total = jnp.sum(ss_ref[...], axis=1, keepdims=True) * (1.0 / LANES)
    return lax.rsqrt(total * (1.0 / d) + EPS)


def _mm_body(*refs, n_pairs, has_res, nk, lead_scale, norm_d, emit_norm):
    pairs = [(refs[2 * p], refs[2 * p + 1]) for p in range(n_pairs)]
    refs = list(refs[2 * n_pairs:])
    res_ref = refs.pop(0) if has_res else None
    ss_in_ref = refs.pop(0) if norm_d else None
    gain_ref = refs.pop(0) if emit_norm else None
    o_ref = refs.pop(0)
    xg_ref, ss_ref = (refs.pop(0), refs.pop(0)) if emit_norm else (None, None)
    part = None
    for a_ref, w_ref in pairs:
        d = jnp.dot(a_ref[...], w_ref[...], preferred_element_type=F32)
        part = d if part is None else part + d

    def finish(total):
        if norm_d:
            total = total * _row_rsqrt(ss_in_ref, norm_d)
        if lead_scale is not None:
            scale, n_tiles = lead_scale
            total = total * jnp.where(pl.program_id(1) < n_tiles, scale, 1.0).astype(F32)
        if has_res:
            total = res_ref[...] + total
        o_ref[...] = total.astype(o_ref.dtype)
        if emit_norm:
            xg_ref[...] = (total * gain_ref[...]).astype(xg_ref.dtype)
            ss_ref[...] = jnp.broadcast_to(jnp.sum(total * total, axis=1, keepdims=True),
                                           ss_ref.shape)

    if nk == 1:
        finish(part)
    else:
        acc_ref = refs.pop(0)
        k = pl.program_id(2)

        @pl.when(k == 0)
        def _():
            acc_ref[...] = part

        @pl.when(k > 0)
        def _():
            acc_ref[...] += part

        @pl.when(k == nk - 1)
        def _():
            finish(acc_ref[...])


def _matmul(pairs, layer, n_out, *, res=None, out_dtype, split_k=False, lead_cols_scale=None,
            row_ss=None, norm_gains=None):
    M = pairs[0][0].shape[0]
    tm = _tile(M, MM_TM)
    tn = _tile(n_out, MM_TN if res is None else MM_TN_F32)
    lead_scale = None
    if lead_cols_scale is not None:
        assert not split_k and lead_cols_scale[1] % tn == 0
        lead_scale = (lead_cols_scale[0], lead_cols_scale[1] // tn)
    if split_k:
        assert len(pairs) == 1
        tk = _tile(pairs[0][0].shape[1], MM_TK)
        tn = _tile(n_out, MM_TN)
        nk = pairs[0][0].shape[1] // tk
    else:
        nk = 1
    in_specs, args = [], []
    for a, w, rb in pairs:
        ka = a.shape[1]
        if nk == 1:
            in_specs.append(pl.BlockSpec((tm, ka), lambda i, j, k: (i, 0)))
            in_specs.append(pl.BlockSpec((None, ka, tn),
                                         lambda i, j, k, rb=rb: (layer, rb, j)))
        else:
            in_specs.append(pl.BlockSpec((tm, tk), lambda i, j, k: (i, k)))
            in_specs.append(pl.BlockSpec((None, tk, tn), lambda i, j, k: (layer, k, j)))
        args += [a, w]
    if res is not None:
        in_specs.append(pl.BlockSpec((tm, tn), lambda i, j, k: (i, j)))
        args.append(res)
    norm_d = 0
    if row_ss is not None:
        assert len(pairs) == 1 and not split_k
        norm_d = pairs[0][0].shape[1]
        in_specs.append(pl.BlockSpec((tm, row_ss.shape[1]), lambda i, j, k: (i, 0)))
        args.append(row_ss)
    tile_spec = pl.BlockSpec((tm, tn), lambda i, j, k: (i, j))
    out_specs, out_shape = tile_spec, jax.ShapeDtypeStruct((M, n_out), out_dtype)
    if norm_gains is not None:
        gains, g_layer = norm_gains
        in_specs.append(pl.BlockSpec((None, 1, tn), lambda i, j, k: (g_layer, 0, j)))
        args.append(gains)
        out_specs = [tile_spec, tile_spec, pl.BlockSpec((tm, LANES), lambda i, j, k: (i, j))]
        out_shape = [out_shape, jax.ShapeDtypeStruct((M, n_out), BF16),
                     jax.ShapeDtypeStruct((M, LANES * (n_out // tn)), F32)]
    scratch = [pltpu.VMEM((tm, tn), F32)] if nk > 1 else []
    return pl.pallas_call(
        functools.partial(_mm_body, n_pairs=len(pairs), has_res=res is not None, nk=nk,
                          lead_scale=lead_scale, norm_d=norm_d, emit_norm=norm_gains is not None),
        grid=(M // tm, n_out // tn, nk),
        in_specs=in_specs,
        out_specs=out_specs,
        out_shape=out_shape,
        scratch_shapes=scratch,
        compiler_params=_params("parallel", "parallel", "arbitrary"),
        name="matmul",
    )(*args)


def _swiglu_body(a_ref, wg_ref, wu_ref, *refs, norm_d):
    o_ref = refs[-1]
    a = a_ref[...]
    g = jnp.dot(a, wg_ref[...], preferred_element_type=F32)
    u = jnp.dot(a, wu_ref[...], preferred_element_type=F32)
    if norm_d:
        r = _row_rsqrt(refs[0], norm_d)
        g, u = g * r, u * r
    o_ref[...] = (jax.nn.silu(g) * u).astype(o_ref.dtype)


def _swiglu_up(a, wg, wu, layer, row_ss=None):
    M, K = a.shape
    N = wg.shape[2]
    tm = _tile(M, MM_TM)
    tn = _tile(N, MM_TN_F32)
    extra = [] if row_ss is None else [row_ss]
    return pl.pallas_call(
        functools.partial(_swiglu_body, norm_d=0 if row_ss is None else K),
        grid=(M // tm, N // tn),
        in_specs=[pl.BlockSpec((tm, K), lambda i, j: (i, 0)),
                  pl.BlockSpec((None, K, tn), lambda i, j: (layer, 0, j)),
                  pl.BlockSpec((None, K, tn), lambda i, j: (layer, 0, j))]
        + [pl.BlockSpec((tm, s.shape[1]), lambda i, j: (i, 0)) for s in extra],
        out_specs=pl.BlockSpec((tm, tn), lambda i, j: (i, j)),
        out_shape=jax.ShapeDtypeStruct((M, N), BF16),
        compiler_params=_params("parallel", "parallel"),
        name="swiglu_up",
    )(a, wg, wu, *extra)


def _lru_body(x_ref, g_ref, cw_ref, cb_ref, wa_ref, ba_ref, wx_ref, bx_ref, lam_ref, o_ref,
              xpad_ref, a_scr, u_scr, h_scr, hstate_ref, *, ts, nblk):
    s = pl.program_id(2)

    @pl.when(s == 0)
    def _():
        xpad_ref[0:CONV_HIST, :] = jnp.zeros((CONV_HIST, xpad_ref.shape[1]), F32)
        hstate_ref[...] = jnp.zeros(hstate_ref.shape, F32)

    xpad_ref[CONV_HIST:CONV_HIST + ts, :] = x_ref[...]
    kw = cw_ref.shape[0]
    xc = cw_ref[kw - 1:kw, :] * x_ref[...]
    for k in range(kw - 1):
        off = CONV_HIST - (kw - 1) + k
        xc = xc + cw_ref[k:k + 1, :] * xpad_ref[off:off + ts, :]
    xc = xc + cb_ref[...]
    xpad_ref[0:CONV_HIST, :] = x_ref[ts - CONV_HIST:ts, :]

    lam = lam_ref[...]
    sp = jnp.maximum(-lam, 0.0) + jnp.log1p(jnp.exp(-jnp.abs(lam)))
    xcb = xc.astype(BF16)
    for n in range(nblk):
        sl = slice(n * LRU_BLOCK, (n + 1) * LRU_BLOCK)
        ga = jnp.dot(xcb[:, sl], wa_ref[n], preferred_element_type=F32) + ba_ref[:, sl]
        gx = jnp.dot(xcb[:, sl], wx_ref[n], preferred_element_type=F32) + bx_ref[:, sl]
        log_a = -LRU_C * jax.nn.sigmoid(ga) * sp[:, sl]
        th = jnp.tanh(log_a)
        mult = jnp.sqrt(-2.0 * th / (1.0 - th))
        a_scr[:, sl] = jnp.exp(log_a)
        u_scr[:, sl] = mult * jax.nn.sigmoid(gx) * xc[:, sl]

    def step(t, h):
        h = a_scr[pl.ds(t, 1), :] * h + u_scr[pl.ds(t, 1), :]
        h_scr[pl.ds(t, 1), :] = h
        return h

    hstate_ref[...] = lax.fori_loop(0, ts, step, hstate_ref[...], unroll=8)
    o_ref[...] = (h_scr[...] * jax.nn.gelu(g_ref[...])).astype(o_ref.dtype)


def _lru(u3, layer, d_lru, conv_w, conv_b, wa, ba, wx, bx, lam):
    B, S, _ = u3.shape
    ts = _tile(S, SEQ_TILE)
    cb = _tile(d_lru, LRU_CB)
    nblk = cb // LRU_BLOCK
    kw = conv_w.shape[1]
    gate_col0 = d_lru // cb
    vec = lambda: pl.BlockSpec((None, 1, cb), lambda b, c, s: (layer, 0, c))
    gate_w = lambda: pl.BlockSpec((None, nblk, LRU_BLOCK, LRU_BLOCK),
                                  lambda b, c, s: (layer, c, 0, 0))
    return pl.pallas_call(
        functools.partial(_lru_body, ts=ts, nblk=nblk),
        grid=(B, d_lru // cb, S // ts),
        in_specs=[pl.BlockSpec((None, ts, cb), lambda b, c, s: (b, s, c)),
                  pl.BlockSpec((None, ts, cb), lambda b, c, s: (b, s, gate_col0 + c)),
                  pl.BlockSpec((None, kw, cb), lambda b, c, s: (layer, 0, c)),
                  vec(), gate_w(), vec(), gate_w(), vec(), vec()],
        out_specs=pl.BlockSpec((None, ts, cb), lambda b, c, s: (b, s, c)),
        out_shape=jax.ShapeDtypeStruct((B, S, d_lru), BF16),
        scratch_shapes=[pltpu.VMEM((CONV_HIST + ts, cb), F32),
                        pltpu.VMEM((ts, cb), F32),
                        pltpu.VMEM((ts, cb), F32),
                        pltpu.VMEM((ts, cb), F32),
                        pltpu.VMEM((1, cb), F32)],
        compiler_params=_params("parallel", "parallel", "arbitrary"),
        name="lru",
    )(u3, u3, conv_w, conv_b, wa, ba, wx, bx, lam)


def _pool_body(x_ref, w_ref, sc_ref, o_ref, xpad_ref, *, ts, gw):
    s = pl.program_id(1)

    @pl.when(s == 0)
    def _():
        xpad_ref[0:POOL_HIST, :] = jnp.zeros((POOL_HIST, xpad_ref.shape[1]), F32)

    xpad_ref[POOL_HIST:POOL_HIST + ts, :] = x_ref[...]
    frames = (s * ts + 1 + lax.broadcasted_iota(jnp.int32, (ts, 1), 0)).astype(F32)
    for g, win in enumerate(POOL_WINDOWS):
        cs = slice(g * gw, (g + 1) * gw)
        acc = x_ref[:, cs]
        for k in range(1, win):
            acc = acc + xpad_ref[POOL_HIST - k:POOL_HIST - k + ts, cs]
        y = acc / jnp.minimum(frames, float(win)) - x_ref[:, cs]
        z = jnp.dot(y.astype(BF16), w_ref[g], preferred_element_type=F32) * sc_ref[:, cs]
        o_ref[:, cs] = z.astype(o_ref.dtype)
    xpad_ref[0:POOL_HIST, :] = x_ref[ts - POOL_HIST:ts, :]


def _pool(u3, layer, col_block, d_pool, w_pool, scale):
    B, S, _ = u3.shape
    ts = _tile(S, SEQ_TILE)
    ng = len(POOL_WINDOWS)
    gw = d_pool // ng
    assert max(POOL_WINDOWS) <= POOL_HIST <= ts
    return pl.pallas_call(
        functools.partial(_pool_body, ts=ts, gw=gw),
        grid=(B, S // ts),
        in_specs=[pl.BlockSpec((None, ts, d_pool), lambda b, s: (b, s, col_block)),
                  pl.BlockSpec((None, ng, gw, gw), lambda b, s: (layer, 0, 0, 0)),
                  pl.BlockSpec((None, 1, d_pool), lambda b, s: (layer, 0, 0))],
        out_specs=pl.BlockSpec((None, ts, d_pool), lambda b, s: (b, s, 0)),
        out_shape=jax.ShapeDtypeStruct((B, S, d_pool), BF16),
        scratch_shapes=[pltpu.VMEM((POOL_HIST + ts, d_pool), F32)],
        compiler_params=_params("parallel", "arbitrary"),
        name="pool",
    )(u3, w_pool, scale)


def _lanes(x, width):
    return jnp.concatenate([x] * (width // LANES), axis=1) if width > LANES else x


def _attn_tile(q_ref, k_ref, v_ref, m_scr, l_scr, acc_scr, *, masked):
    hw = acc_scr.shape[2]
    for r in range(m_scr.shape[0]):
        head, c = divmod(r, 2)
        cs = slice(head * hw + c * HEAD_DIM, head * hw + (c + 1) * HEAD_DIM)
        s = lax.dot_general(q_ref[:, cs], k_ref[:, cs], (((1,), (1,)), ((), ())),
                            preferred_element_type=F32)
        if masked:
            shift = CHUNK.bit_length() - 1
            qc = lax.shift_right_logical(lax.broadcasted_iota(jnp.int32, s.shape, 0), shift)
            kc = lax.shift_right_logical(lax.broadcasted_iota(jnp.int32, s.shape, 1), shift)
            s = jnp.where(kc <= qc, s, NEG_INF)
        m_prev = m_scr[r]
        m_new = jnp.maximum(m_prev, jnp.max(s, axis=1, keepdims=True))
        alpha = jnp.exp2(m_prev - m_new)
        p = jnp.exp2(s - _lanes(m_new, s.shape[1]))
        l_scr[r] = alpha * l_scr[r] + jnp.sum(p, axis=1, keepdims=True)
        acc_scr[r] = (_lanes(alpha, hw) * acc_scr[r]
                      + jnp.dot(p.astype(BF16), v_ref[:, head * hw:(head + 1) * hw],
                                preferred_element_type=F32))
        m_scr[r] = m_new


def _attn_body(qi_ref, kj_ref, lam_ref, g_ref, q_ref, k_ref, v_ref, o_ref, m_scr, l_scr, acc_scr,
               *, lambda_init):
    step = pl.program_id(2)
    i = qi_ref[step]
    j = kj_ref[step]

    @pl.when(j == 0)
    def _():
        m_scr[...] = jnp.full(m_scr.shape, NEG_INF, F32)
        l_scr[...] = jnp.zeros(l_scr.shape, F32)
        acc_scr[...] = jnp.zeros(acc_scr.shape, F32)

    tile = functools.partial(_attn_tile, q_ref, k_ref, v_ref, m_scr, l_scr, acc_scr)

    @pl.when(j < i)
    def _():
        tile(masked=False)

    @pl.when(j == i)
    def _():
        tile(masked=True)
        lv = lam_ref[...]
        lam = (jnp.exp(jnp.sum(lv[0:1] * lv[1:2], axis=1, keepdims=True))
               - jnp.exp(jnp.sum(lv[2:3] * lv[3:4], axis=1, keepdims=True)) + lambda_init)
        hw = acc_scr.shape[2]
        for head in range(m_scr.shape[0] // 2):
            r = 2 * head
            o = (acc_scr[r] / _lanes(l_scr[r], hw)
                 - lam * (acc_scr[r + 1] / _lanes(l_scr[r + 1], hw)))
            o_ref[:, head * hw:(head + 1) * hw] = (
                _rms(o, g_ref[...]) * (1.0 - lambda_init)).astype(o_ref.dtype)


def _attention(qkv3, layer, lam_vecs, subln_g, lambda_init):
    B, S, d3 = qkv3.shape
    D = d3 // 3
    hw = 2 * HEAD_DIM
    H = D // hw
    hps = _tile(H, ATTN_HEADS)
    bw = hps * hw
    nb = H // hps
    t = _tile(S, ATTN_TILE)
    assert CHUNK & (CHUNK - 1) == 0 and t % CHUNK == 0
    pairs = [(i, j) for i in range(S // t) for j in range(i + 1)]
    qi = jnp.asarray(np.array([p[0] for p in pairs], np.int32))
    kj = jnp.asarray(np.array([p[1] for p in pairs], np.int32))
    return pl.pallas_call(
        functools.partial(_attn_body, lambda_init=lambda_init),
        grid_spec=pltpu.PrefetchScalarGridSpec(
            num_scalar_prefetch=2,
            grid=(B, nb, len(pairs)),
            in_specs=[pl.BlockSpec((None, 4, HEAD_DIM), lambda b, h, s, qi, kj: (layer, 0, 0)),
                      pl.BlockSpec((None, 1, hw), lambda b, h, s, qi, kj: (layer, 0, 0)),
                      pl.BlockSpec((None, t, bw), lambda b, h, s, qi, kj: (b, qi[s], h)),
                      pl.BlockSpec((None, t, bw), lambda b, h, s, qi, kj: (b, kj[s], nb + h)),
                      pl.BlockSpec((None, t, bw), lambda b, h, s, qi, kj: (b, kj[s], 2 * nb + h))],
            out_specs=pl.BlockSpec((None, t, bw), lambda b, h, s, qi, kj: (b, qi[s], h)),
            scratch_shapes=[pltpu.VMEM((2 * hps, t, LANES), F32),
                            pltpu.VMEM((2 * hps, t, LANES), F32),
                            pltpu.VMEM((2 * hps, t, hw), F32)]),
        out_shape=jax.ShapeDtypeStruct((B, S, D), BF16),
        compiler_params=_params("parallel", "parallel", "arbitrary"),
        name="diff_attention",
    )(qi, kj, lam_vecs, subln_g, qkv3, qkv3, qkv3)


ROUTE_IDX0, ROUTE_IDX1, ROUTE_W0, ROUTE_W1 = 0, 1, 2, 3


def _pack_halves(x):
    n = x.shape[1] // 2
    bits = lax.bitcast_convert_type(x.astype(jnp.bfloat16).astype(F32), jnp.uint32)
    return bits[:, n:] | lax.shift_right_logical(bits[:, :n], jnp.uint32(16))


def _unpack_halves(w):
    lo = lax.bitcast_convert_type(lax.shift_left(w, jnp.uint32(16)), F32)
    hi = lax.bitcast_convert_type(w & jnp.uint32(0xFFFF0000), F32)
    return lo, hi


def _router_body(x_ref, g_ref, whi_ref, wlo_ref, h_ref, route_ref, *, n_experts):
    h = _rms(x_ref[...], g_ref[...])
    h_ref[...] = _pack_halves(h)
    h_hi = h.astype(BF16)
    h_lo = (h - h_hi.astype(F32)).astype(BF16)
    logits = (jnp.dot(h_hi, whi_ref[...], preferred_element_type=F32)
              + (jnp.dot(h_hi, wlo_ref[...], preferred_element_type=F32)
                 + jnp.dot(h_lo, whi_ref[...], preferred_element_type=F32)))
    col = lax.broadcasted_iota(jnp.int32, logits.shape, 1)
    logits = jnp.where(col < n_experts, logits, -jnp.inf)
    v0 = jnp.max(logits, axis=1, keepdims=True)
    i0 = jnp.min(jnp.where(logits == v0, col, LANES), axis=1, keepdims=True)
    rest = jnp.where(col == i0, -jnp.inf, logits)
    v1 = jnp.max(rest, axis=1, keepdims=True)
    i1 = jnp.min(jnp.where(rest == v1, col, LANES), axis=1, keepdims=True)
    e = jnp.exp(v1 - v0)
    w0 = 1.0 / (1.0 + e)
    w1 = e / (1.0 + e)
    route = jnp.where(col == ROUTE_IDX0, i0.astype(F32), 0.0)
    route = jnp.where(col == ROUTE_IDX1, i1.astype(F32), route)
    route = jnp.where(col == ROUTE_W0, w0, route)
    route = jnp.where(col == ROUTE_W1, w1, route)
    route_ref[...] = route


def _norm_router(x, g, w_router, n_experts):
    T, D = x.shape
    tr = _tile(T, ROW_TILE)
    wr = jnp.pad(w_router, ((0, 0), (0, LANES - n_experts)))
    w_hi = wr.astype(BF16)
    w_lo = (wr - w_hi.astype(F32)).astype(BF16)
    return pl.pallas_call(
        functools.partial(_router_body, n_experts=n_experts),
        grid=(T // tr,),
        in_specs=[pl.BlockSpec((tr, D), lambda i: (i, 0)),
                  pl.BlockSpec((1, D), lambda i: (0, 0)),
                  pl.BlockSpec((D, LANES), lambda i: (0, 0)),
                  pl.BlockSpec((D, LANES), lambda i: (0, 0))],
        out_specs=[pl.BlockSpec((tr, D // 2), lambda i: (i, 0)),
                   pl.BlockSpec((tr, LANES), lambda i: (i, 0))],
        out_shape=[jax.ShapeDtypeStruct((T, D // 2), jnp.uint32),
                   jax.ShapeDtypeStruct((T, LANES), F32)],
        compiler_params=_params("parallel"),
        name="norm_router",
    )(x, g.reshape(1, D), w_hi, w_lo)


def _slot_plan(expert_idx, n_experts, tm):
    T = expert_idx.shape[0]
    flat = expert_idx.reshape(T * TOP_K)
    onehot = (flat[:, None] == jnp.arange(n_experts, dtype=jnp.int32)[None, :]).astype(jnp.int32)
    csum = jnp.cumsum(onehot, axis=0)
    pos = jnp.sum(csum * onehot, axis=1) - 1
    counts = csum[-1]
    padded = (counts + tm - 1) // tm * tm
    gend = jnp.cumsum(padded)
    gstart = gend - padded
    slot = (jnp.sum(onehot * gstart[None, :], axis=1) + pos).astype(jnp.int32)
    n_tiles = (T * TOP_K) // tm + n_experts
    tile_start = jnp.arange(n_tiles, dtype=jnp.int32) * tm
    tile_expert = jnp.sum((tile_start[:, None] >= gend[None, :]).astype(jnp.int32), axis=1)
    n_valid = (gend[-1] // tm).astype(jnp.int32).reshape(1)
    tile_expert = jnp.minimum(tile_expert, n_experts - 1).astype(jnp.int32)
    token = jnp.arange(T * TOP_K, dtype=jnp.int32) // TOP_K
    src = jnp.zeros((n_tiles * tm,), jnp.int32).at[slot].set(token)
    return slot, src, tile_expert, n_valid, n_tiles


def _row_copies(index, n_rows, src_ref, dst_ref, sem):
    def copy(r):
        return pltpu.make_async_copy(src_ref.at[pl.ds(index(r), 1), :],
                                     dst_ref.at[pl.ds(r, 1), :], sem)

    def start_one(r, carry):
        copy(r).start()
        return carry

    def wait_one(r, carry):
        copy(r).wait()
        return carry

    start = lambda: lax.fori_loop(0, n_rows, start_one, 0, unroll=8)
    wait = lambda: lax.fori_loop(0, n_rows, wait_one, 0, unroll=8)
    return start, wait


def _gather_slots_body(src_ref, nv_ref, h_ref, o_ref, sem, *, tm):
    i = pl.program_id(0)
    valid = i < nv_ref[0]

    @pl.when(valid)
    def _():
        start, wait = _row_copies(lambda r: src_ref[i * tm + r], tm, h_ref, o_ref, sem)
        start()
        wait()

    @pl.when(jnp.logical_not(valid))
    def _():
        o_ref[...] = jnp.zeros(o_ref.shape, o_ref.dtype)


def _gather_slots(src, n_valid, hp, n_tiles, tm):
    _, W = hp.shape
    return pl.pallas_call(
        functools.partial(_gather_slots_body, tm=tm),
        grid_spec=pltpu.PrefetchScalarGridSpec(
            num_scalar_prefetch=2,
            grid=(n_tiles,),
            in_specs=[pl.BlockSpec(memory_space=pl.ANY)],
            out_specs=pl.BlockSpec((tm, W), lambda i, src, nv: (i, 0)),
            scratch_shapes=[pltpu.SemaphoreType.DMA(())]),
        out_shape=jax.ShapeDtypeStruct((n_tiles * tm, W), hp.dtype),
        compiler_params=_params("arbitrary"),
        name="gather_slots",
    )(src, n_valid, hp)


def _moe_up_body(a_ref, wg_ref, wu_ref, o_ref):
    lo, hi = _unpack_halves(a_ref[...])
    a = jnp.concatenate([lo.astype(BF16), hi.astype(BF16)], axis=1)
    g = jnp.dot(a, wg_ref[...], preferred_element_type=F32)
    u = jnp.dot(a, wu_ref[...], preferred_element_type=F32)
    o_ref[...] = (jax.nn.silu(g) * u).astype(o_ref.dtype)


def _moe_down_body(a_ref, w_ref, o_ref):
    o_ref[...] = _pack_halves(jnp.dot(a_ref[...], w_ref[...], preferred_element_type=F32))


def _grouped_body(te_ref, nv_ref, *refs, body, n_cast):
    i = pl.program_id(1)
    valid = i < nv_ref[0]
    if n_cast:
        refs, w_scr = refs[:-n_cast], refs[-n_cast:]
        a_ref, w_refs, o_ref = refs[0], refs[1:-1], refs[-1]
        fresh = jnp.logical_or(i == 0, te_ref[i] != te_ref[jnp.maximum(i - 1, 0)])

        @pl.when(jnp.logical_and(valid, fresh))
        def _():
            for w_ref, scr in zip(w_refs, w_scr):
                scr[...] = w_ref[...].astype(scr.dtype)

        refs = (a_ref, *w_scr, o_ref)
    o_ref = refs[-1]

    @pl.when(valid)
    def _():
        body(*refs)

    @pl.when(jnp.logical_not(valid))
    def _():
        o_ref[...] = jnp.zeros(o_ref.shape, o_ref.dtype)


def _grouped_call(body, tile_expert, n_valid, a, weights, layer, n_tiles, tm, name, *, packed_out):
    P, ka = a.shape
    K, N = weights[0].shape[2:]
    tn = _tile(N, MM_TN)
    pack = 2 if packed_out else 1
    n_cast = len(weights) if weights[0].dtype == F32 else 0
    row = lambda n, i, te, nv: jnp.maximum(jnp.minimum(i, nv[0] - 1), 0)
    w_spec = pl.BlockSpec((None, None, K, tn),
                          lambda n, i, te, nv: (layer, te[row(n, i, te, nv)], 0, n))
    return pl.pallas_call(
        functools.partial(_grouped_body, body=body, n_cast=n_cast),
        grid_spec=pltpu.PrefetchScalarGridSpec(
            num_scalar_prefetch=2,
            grid=(N // tn, n_tiles),
            in_specs=[pl.BlockSpec((tm, ka), lambda n, i, te, nv: (row(n, i, te, nv), 0))]
            + [w_spec] * len(weights),
            out_specs=pl.BlockSpec((tm, tn // pack), lambda n, i, te, nv: (i, n)),
            scratch_shapes=[pltpu.VMEM((K, tn), BF16)] * n_cast),
        out_shape=jax.ShapeDtypeStruct((P, N // pack), jnp.uint32 if packed_out else BF16),
        compiler_params=_params("parallel", "arbitrary"),
        name=name,
    )(tile_expert, n_valid, a, *weights)


def _combine_body(slot_ref, x_ref, route_ref, y_ref, g_ref, *rest, emit_x, tn):
    out_refs, (ybuf, sem) = rest[:-2], rest[-2:]
    tr = x_ref.shape[0]
    base = pl.program_id(0) * tr
    loops = [_row_copies(lambda r, k=k: slot_ref[TOP_K * (base + r) + k], tr, y_ref, ybuf.at[k], sem)
             for k in range(TOP_K)]
    for start, _ in loops:
        start()
    for _, wait in loops:
        wait()
    half = tn // 2
    y = None
    for k, col in enumerate((ROUTE_W0, ROUTE_W1)):
        lo, hi = _unpack_halves(ybuf[k])
        parts = []
        for n in range(lo.shape[1] // half):
            parts += [lo[:, n * half:(n + 1) * half], hi[:, n * half:(n + 1) * half]]
        yk = route_ref[:, col:col + 1] * jnp.concatenate(parts, axis=1)
        y = yk if y is None else y + yk
    x = x_ref[...] + y
    if emit_x:
        out_refs[0][...] = x
    h_ref = out_refs[-1]
    h_ref[...] = _rms(x, g_ref[...]).astype(h_ref.dtype)


def _combine_norm(slot, x, route, yp, g, tn, *, emit_x, h_dtype):
    T, D = x.shape
    tr = _tile(T, ROW_TILE)
    row = pl.BlockSpec((tr, D), lambda i, slot: (i, 0))
    outs = ([jax.ShapeDtypeStruct((T, D), F32)] if emit_x else []) + \
        [jax.ShapeDtypeStruct((T, D), h_dtype)]
    return pl.pallas_call(
        functools.partial(_combine_body, emit_x=emit_x, tn=tn),
        grid_spec=pltpu.PrefetchScalarGridSpec(
            num_scalar_prefetch=1,
            grid=(T // tr,),
            in_specs=[row,
                      pl.BlockSpec((tr, LANES), lambda i, slot: (i, 0)),
                      pl.BlockSpec(memory_space=pl.ANY),
                      pl.BlockSpec((1, D), lambda i, slot: (0, 0))],
            out_specs=[row] * len(outs),
            scratch_shapes=[pltpu.VMEM((TOP_K, tr, D // 2), jnp.uint32),
                            pltpu.SemaphoreType.DMA(())]),
        out_shape=outs,
        compiler_params=_params("arbitrary"),
        name="combine_norm",
    )(slot, x, route, yp, g.reshape(1, D))


def _moe(x, norm_g, w_router, w_gate, w_up, w_down, layer, next_g, *, emit_x, h_dtype):
    T, D = x.shape
    n_experts = w_router.shape[2]
    tm = _tile(T * TOP_K // n_experts, MOE_TM)
    hp, route = _norm_router(x, norm_g, w_router[layer], n_experts)
    expert_idx = route[:, ROUTE_IDX0:ROUTE_IDX1 + 1].astype(jnp.int32)
    slot, src, tile_expert, n_valid, n_tiles = _slot_plan(expert_idx, n_experts, tm)
    xg = _gather_slots(src, n_valid, hp, n_tiles, tm)
    act = _grouped_call(_moe_up_body, tile_expert, n_valid, xg, [w_gate, w_up], layer,
                        n_tiles, tm, "moe_up", packed_out=False)
    yp = _grouped_call(_moe_down_body, tile_expert, n_valid, act, [w_down], layer,
                       n_tiles, tm, "moe_down", packed_out=True)
    return _combine_norm(slot, x, route, yp, next_g, _tile(D, MM_TN), emit_x=emit_x, h_dtype=h_dtype)


def kernel(x, ev_norm1, ev_w_in, ev_conv_w, ev_conv_b, ev_gate_a_w, ev_gate_a_b, ev_gate_x_w, ev_gate_x_b, ev_lru_lambda, ev_pool_w, ev_pool_scale, ev_w_out, ev_norm2, ev_ffn_gate, ev_ffn_up, ev_ffn_down, od_norm1, od_w_qkv, od_lambda, od_subln, od_w_o, od_norm2, od_router, od_exp_gate, od_exp_up, od_exp_down, final_norm):
    B, S, D = x.shape
    T = B * S
    n_even, n_odd = ev_norm1.shape[0], od_norm1.shape[0]
    depth = n_even + n_odd
    d_lru = ev_lru_lambda.shape[1]
    d_pool = ev_pool_scale.shape[1]

    w_in, w_out = ev_w_in.astype(BF16), ev_w_out.astype(BF16)
    gate_a_w, gate_x_w = ev_gate_a_w.astype(BF16), ev_gate_x_w.astype(BF16)
    pool_w = ev_pool_w.astype(BF16)
    ffn_gate, ffn_up, ffn_down = (w.astype(BF16) for w in (ev_ffn_gate, ev_ffn_up, ev_ffn_down))
    w_qkv, w_o = od_w_qkv.astype(BF16), od_w_o.astype(BF16)
    exp_gate, exp_up = od_exp_gate.astype(BF16), od_exp_up.astype(BF16)
    exp_down = od_exp_down
    row3 = lambda v: v.reshape(v.shape[0], 1, v.shape[1])

    xs = x.reshape(T, D)
    h = None
    split_norm = None
    out = None
    for layer in range(depth):
        j = layer // 2
        if layer % 2 == 0:
            assert split_norm is None
            if h is None:
                h = _rmsnorm(xs, ev_norm1[j], BF16)
            u3 = _matmul([(h, w_in, 0)], j, w_in.shape[2], out_dtype=F32).reshape(B, S, -1)
            y_lru = _lru(u3, j, d_lru, ev_conv_w, row3(ev_conv_b), gate_a_w, row3(ev_gate_a_b),
                         gate_x_w, row3(ev_gate_x_b), row3(ev_lru_lambda))
            assert (2 * d_lru) % d_pool == 0
            y_pool = _pool(u3, j, 2 * d_lru // d_pool, d_pool, pool_w, row3(ev_pool_scale))
            assert d_lru == d_pool
            xs, xg, ss = _matmul(
                [(y_lru.reshape(T, d_lru), w_out, 0), (y_pool.reshape(T, d_pool), w_out, 1)],
                j, D, res=xs, out_dtype=F32, norm_gains=(row3(ev_norm2), j))
            act = _swiglu_up(xg, ffn_gate, ffn_up, j, row_ss=ss)
            h = None
            if layer + 1 < depth:
                xs, xg, ss = _matmul([(act, ffn_down, 0)], j, D, res=xs, out_dtype=F32, split_k=True,
                                     norm_gains=(row3(od_norm1), j))
                split_norm = (xg, ss)
            else:
                xs = _matmul([(act, ffn_down, 0)], j, D, res=xs, out_dtype=F32, split_k=True)
                split_norm = None
        else:
            lambda_init = 0.8 - 0.6 * math.exp(-0.3 * layer)
            q_scale = (HEAD_DIM ** -0.5 * math.log2(math.e), D)
            if split_norm is not None:
                qkv = _matmul([(split_norm[0], w_qkv, 0)], j, 3 * D, out_dtype=BF16,
                              lead_cols_scale=q_scale, row_ss=split_norm[1])
            else:
                if h is None:
                    h = _rmsnorm(xs, od_norm1[j], BF16)
                qkv = _matmul([(h, w_qkv, 0)], j, 3 * D, out_dtype=BF16, lead_cols_scale=q_scale)
            qkv3 = qkv.reshape(B, S, 3 * D)
            split_norm = None
            o = _attention(qkv3, j, od_lambda, row3(od_subln), lambda_init)
            xs = _matmul([(o.reshape(T, D), w_o, 0)], j, D, res=xs, out_dtype=F32)
            last = layer == depth - 1
            if last:
                next_g = final_norm
            else:
                next_g = ev_norm1[j + 1]
            res = _moe(xs, od_norm2[j], od_router, exp_gate, exp_up, exp_down, j, next_g,
                       emit_x=not last, h_dtype=F32 if last else BF16)
            if last:
                out = res[0]
            else:
                xs, h = res
    if out is None:
        out = _rmsnorm(xs, final_norm, F32)
    return out.reshape(B, S, D)
```
